```python
import jax
import jax.numpy as jnp
from jax import lax
import numpy as np

D_MODEL = 2048
BATCH = 4
SEQ = 4096
DEPTH = 2

CTX_LEN = 256
GRID_W = 64

ATT_HEADS = 8
ATT_KV_HEADS = 2
ATT_HEAD_DIM = 128
ATT_BLOCK = 128
ROPE_THETA = 10000.0

ML_HEADS = 4
ML_QK_DIM = 128
ML_V_DIM = 256
ML_CHUNK = 128

POOL_WINDOWS = (2, 4, 8, 16)
POOL_GROUP = 256

N_BRANCH = 3
D_FF = -(-(8 * D_MODEL) // (3 * 256)) * 256
EPS = 1e-6

ATT_Q_W = ATT_HEADS * ATT_HEAD_DIM
ATT_KV_W = ATT_KV_HEADS * ATT_HEAD_DIM
ML_QK_W = ML_HEADS * ML_QK_DIM
ML_V_W = ML_HEADS * ML_V_DIM
ML_GATE_W = 2 * ML_HEADS
POOL_W = len(POOL_WINDOWS) * POOL_GROUP
GATE_W = N_BRANCH * D_MODEL
PROJ_WIDTHS = (ATT_KV_W, ATT_KV_W, ML_QK_W, ML_V_W, ML_GATE_W, ML_GATE_W,
               ATT_Q_W, ML_QK_W, ML_V_W, POOL_W, GATE_W)
PROJ_SPLITS = tuple(int(s) for s in np.cumsum(PROJ_WIDTHS)[:-1])
CTX_SIDE_W = PROJ_SPLITS[5]
D_IN = int(sum(PROJ_WIDTHS))

kernel_name = 'hybrid_gqa_mlstm_pool_dit_block'


def _rmsnorm(x, g):
    xf = x.astype(jnp.float32)
    y = xf * lax.rsqrt(jnp.mean(xf * xf, axis=-1, keepdims=True) + EPS)
    return (y * g.astype(jnp.float32)).astype(x.dtype)


def _modulate(h, shift, scale):
    return h * (1 + scale) + shift


def _axial_rope_tables(n_tokens):
    rows_n = n_tokens // GRID_W
    row = jnp.repeat(jnp.arange(rows_n), GRID_W).astype(jnp.float32)
    col = jnp.tile(jnp.arange(GRID_W), rows_n).astype(jnp.float32)
    n_freq = ATT_HEAD_DIM // 4
    inv = ROPE_THETA ** (-jnp.arange(n_freq, dtype=jnp.float32) / n_freq)
    ang = jnp.stack([row[:, None] * inv, col[:, None] * inv], axis=1)
    return jnp.cos(ang), jnp.sin(ang)


def _apply_rope(t, cos, sin):
    B, S, H, hd = t.shape
    tr = t.astype(jnp.float32).reshape(B, S, H, 2, 2, hd // 4)
    a, b = tr[..., 0, :], tr[..., 1, :]
    cs, sn = cos[None, :, None], sin[None, :, None]
    out = jnp.stack([a * cs - b * sn, b * cs + a * sn], axis=-2)
    return out.reshape(B, S, H, hd).astype(t.dtype)


def _gqa_blocks(q, k, v):
    B, Sq, H, hd = q.shape
    G = H // ATT_KV_HEADS
    nb = Sq // ATT_BLOCK
    qb = jnp.moveaxis(q.reshape(B, nb, ATT_BLOCK, ATT_KV_HEADS, G, hd), 1, 0)
    scale = hd ** -0.5

    def one_block(qblk):
        s = jnp.einsum('bqkgd,bskd->bkgqs', qblk, k).astype(jnp.float32) * scale
        pr = jax.nn.softmax(s, axis=-1).astype(v.dtype)
        return jnp.einsum('bkgqs,bskd->bqkgd', pr, v)

    out = lax.map(one_block, qb)
    return jnp.moveaxis(out, 0, 1).reshape(B, Sq, H * hd)


def _chunk(a):
    return a.reshape(a.shape[0], a.shape[1] // ML_CHUNK, ML_CHUNK, *a.shape[2:])


def _zero_state(B):
    return (jnp.zeros((B, ML_HEADS, ML_V_DIM, ML_QK_DIM), jnp.float32),
            jnp.zeros((B, ML_HEADS, ML_QK_DIM), jnp.float32),
            jnp.zeros((B, ML_HEADS), jnp.float32))


def _mlstm_gates(i_raw, f_raw, gate_b):
    B, T, _ = i_raw.shape
    li = i_raw.reshape(B, T, 2, ML_HEADS).astype(jnp.float32) + gate_b[:, 0]
    lf = jax.nn.log_sigmoid(f_raw.reshape(B, T, 2, ML_HEADS).astype(jnp.float32) + gate_b[:, 1])
    return li, lf


def _mlstm_states(k, v, li, lf, state0):
    kc, vc, lic = _chunk(k), _chunk(v), _chunk(li)
    b = jnp.cumsum(_chunk(lf), axis=2)
    g = b[:, :, -1]
    w_end = g[:, :, None] - b + lic
    m_loc = jnp.max(w_end, axis=2)
    e = jnp.exp(w_end - m_loc[:, :, None])
    c_loc = jnp.einsum('bclh,bclhv,bclhk->bchvk', e, vc, kc)
    n_loc = jnp.einsum('bclh,bclhk->bchk', e, kc)

    def step(carry, inp):
        C, n, m = carry
        gc, mc, Cc, ncc = inp
        m_new = jnp.maximum(gc + m, mc)
        a = jnp.exp(gc + m - m_new)
        s = jnp.exp(mc - m_new)
        new = (a[..., None, None] * C + s[..., None, None] * Cc,
               a[..., None] * n + s[..., None] * ncc, m_new)
        return new, (C, n, m)

    seq = tuple(jnp.moveaxis(t, 1, 0) for t in (g, m_loc, c_loc, n_loc))
    final, entry = lax.scan(step, state0, seq)
    entry = tuple(jnp.moveaxis(t, 0, 1) for t in entry)
    return final, entry, b, lic


def _mlstm_outputs(q, k, v, b, lic, entry):
    B, T, H, _ = q.shape
    c_in, n_in, m_in = entry
    qc, kc, vc = _chunk(q), _chunk(k), _chunk(v)
    L = ML_CHUNK
    a_log = b + m_in[:, :, None]
    d_log = b[:, :, :, None] - b[:, :, None] + lic[:, :, None]
    lower = (jnp.arange(L)[:, None] >= jnp.arange(L)[None, :])[None, None, :, :, None]
    d_log = jnp.where(lower, d_log, -jnp.inf)
    m_j = jnp.maximum(a_log, jnp.max(d_log, axis=3))
    w = jnp.exp(d_log - m_j[:, :, :, None]) * jnp.einsum('bcjhd,bcshd->bcjsh', qc, kc)
    num = jnp.einsum('bcjsh,bcshv->bcjhv', w, vc)
    den = jnp.sum(w, axis=3)
    inter = jnp.exp(a_log - m_j)
    num = num + inter[..., None] * jnp.einsum('bchvk,bcjhk->bcjhv', c_in, qc)
    den = den + inter * jnp.einsum('bchk,bcjhk->bcjh', n_in, qc)
    h = num / jnp.maximum(jnp.abs(den), jnp.exp(-m_j))[..., None]
    return h.reshape(B, T, H, ML_V_DIM)


def _mlstm_direction(q, k, v, li, lf, state0, reverse, with_output):
    if reverse:
        k, v, li, lf = (jnp.flip(t, axis=1) for t in (k, v, li, lf))
    final, entry, b, lic = _mlstm_states(k, v, li, lf, state0)
    if not with_output:
        return None, final
    qd = jnp.flip(q, axis=1) if reverse else q
    h = _mlstm_outputs(qd, k, v, b, lic, entry)
    return (jnp.flip(h, axis=1) if reverse else h), final


def _mlstm_inputs(q_raw, k_raw, v_raw):
    B, T, _ = k_raw.shape
    k = k_raw.reshape(B, T, ML_HEADS, ML_QK_DIM).astype(jnp.float32) * (ML_QK_DIM ** -0.5)
    v = v_raw.reshape(B, T, ML_HEADS, ML_V_DIM).astype(jnp.float32)
    q = None if q_raw is None else q_raw.reshape(B, T, ML_HEADS, ML_QK_DIM).astype(jnp.float32)
    return q, k, v


def _pool_branch(u, pool_w, pool_scale):
    B, T, _ = u.shape
    uf = u.astype(jnp.float32)
    cs = jnp.concatenate([jnp.zeros((B, 1, POOL_W), jnp.float32), jnp.cumsum(uf, axis=1)], axis=1)
    t = jnp.arange(T)
    means = []
    for gi, w in enumerate(POOL_WINDOWS):
        lo = jnp.clip(t - w // 2, 0, T)
        hi = jnp.clip(t + w // 2, 0, T)
        seg = cs[:, :, gi * POOL_GROUP:(gi + 1) * POOL_GROUP]
        means.append((seg[:, hi] - seg[:, lo]) / (hi - lo).astype(jnp.float32)[None, :, None])
    mixed = (jnp.concatenate(means, axis=-1) - uf).astype(u.dtype)
    mixed = jnp.einsum('btgc,gcd->btgd', mixed.reshape(B, T, len(POOL_WINDOWS), POOL_GROUP), pool_w)
    return mixed.reshape(B, T, POOL_W) * pool_scale


def _merge(att, h_ml, o_raw, pool_raw, gate_raw, ml_head_g, pool_w, pool_scale,
           w_up_att, w_up_ml, w_up_pool, w_out):
    B, T, _ = o_raw.shape
    ml = _rmsnorm(h_ml, ml_head_g.reshape(ML_HEADS, ML_V_DIM)).reshape(B, T, ML_V_W).astype(o_raw.dtype)
    ml = ml * jax.nn.sigmoid(o_raw)
    pool = _pool_branch(pool_raw, pool_w, pool_scale)
    gts = jax.nn.sigmoid(gate_raw).reshape(B, T, N_BRANCH, D_MODEL)
    y = (gts[:, :, 0] * (att @ w_up_att) + gts[:, :, 1] * (ml @ w_up_ml)
         + gts[:, :, 2] * (pool @ w_up_pool))
    return y @ w_out


def _token_mixers(h, hc, cos, sin, w_in, ml_gate_b, qk_norm_g, ml_head_g, pool_w, pool_scale,
                  w_up_att, w_up_ml, w_up_pool, w_out, ctx_out):
    B, S, _ = h.shape
    Sc = hc.shape[1]
    p = jnp.split(h @ w_in, PROJ_SPLITS, axis=-1)
    if ctx_out:
        pc = jnp.split(hc @ w_in, PROJ_SPLITS, axis=-1)
    else:
        pc = jnp.split(hc @ w_in[:, :CTX_SIDE_W], PROJ_SPLITS[:5], axis=-1)

    att_kc = _rmsnorm(pc[0].reshape(B, Sc, ATT_KV_HEADS, ATT_HEAD_DIM), qk_norm_g[1])
    att_vc = pc[1].reshape(B, Sc, ATT_KV_HEADS, ATT_HEAD_DIM)
    ml_qc, ml_kc, ml_vc = _mlstm_inputs(pc[7] if ctx_out else None, pc[2], pc[3])
    li_c, lf_c = _mlstm_gates(pc[4], pc[5], ml_gate_b)
    ctx_h, ctx_final = [], []
    for d in range(2):
        hd_, fin = _mlstm_direction(ml_qc, ml_kc, ml_vc, li_c[:, :, d], lf_c[:, :, d],
                                    _zero_state(B), d == 1, ctx_out)
        ctx_h.append(hd_)
        ctx_final.append(fin)

    att_q = _apply_rope(_rmsnorm(p[6].reshape(B, S, ATT_HEADS, ATT_HEAD_DIM), qk_norm_g[0]), cos, sin)
    att_k = _apply_rope(_rmsnorm(p[0].reshape(B, S, ATT_KV_HEADS, ATT_HEAD_DIM), qk_norm_g[1]), cos, sin)
    att_v = p[1].reshape(B, S, ATT_KV_HEADS, ATT_HEAD_DIM)
    att = _gqa_blocks(att_q, jnp.concatenate([att_k, att_kc], axis=1),
                      jnp.concatenate([att_v, att_vc], axis=1))
    ml_q, ml_k, ml_v = _mlstm_inputs(p[7], p[2], p[3])
    li, lf = _mlstm_gates(p[4], p[5], ml_gate_b)
    h_f, _ = _mlstm_direction(ml_q, ml_k, ml_v, li[:, :, 0], lf[:, :, 0], ctx_final[0], False, True)
    h_b, _ = _mlstm_direction(ml_q, ml_k, ml_v, li[:, :, 1], lf[:, :, 1], ctx_final[1], True, True)
    y = _merge(att, h_f + h_b, p[8], p[9], p[10], ml_head_g, pool_w, pool_scale,
               w_up_att, w_up_ml, w_up_pool, w_out)
    if not ctx_out:
        return y, None

    att_qc = _rmsnorm(pc[6].reshape(B, Sc, ATT_HEADS, ATT_HEAD_DIM), qk_norm_g[0])
    att_ctx = _gqa_blocks(att_qc, att_kc, att_vc)
    yc = _merge(att_ctx, ctx_h[0] + ctx_h[1], pc[8], pc[9], pc[10], ml_head_g, pool_w, pool_scale,
                w_up_att, w_up_ml, w_up_pool, w_out)
    return y, yc


def _swiglu(h, w_in, w_out):
    gt, up = jnp.split(h @ w_in, 2, axis=-1)
    return (jax.nn.silu(gt) * up) @ w_out


def setup_inputs(seed: int = 0) -> dict:
    key = jax.random.key(seed)
    ks = jax.random.split(key, 20)
    f32 = jnp.float32
    D = D_MODEL

    def nrm(k, shape, scale):
        return jax.random.normal(k, shape, f32) * scale

    head_frac = jnp.arange(ML_HEADS, dtype=f32) / max(ML_HEADS - 1, 1)
    i_b = nrm(ks[8], (DEPTH, 2, ML_HEADS), 0.1)
    f_b = 3.0 + 3.0 * head_frac + nrm(ks[9], (DEPTH, 2, ML_HEADS), 0.1)
    return {
        'x': nrm(ks[0], (BATCH, SEQ, D), 1.0),
        'c': nrm(ks[1], (BATCH, D), 1.0),
        'ctx': nrm(ks[2], (BATCH, CTX_LEN, D), 1.0),
        'c_ctx': nrm(ks[3], (D,), 1.0),
        'w_mod': nrm(ks[4], (DEPTH, D, 6 * D), 0.5 * D ** -0.5),
        'b_mod': nrm(ks[5], (DEPTH, 6 * D), 0.02),
        'norm_g': 1.0 + nrm(ks[6], (DEPTH, 4, D), 0.05),
        'w_in': nrm(ks[7], (DEPTH, D, D_IN), D ** -0.5),
        'ml_gate_b': jnp.stack([i_b, f_b], axis=2),
        'qk_norm_g': 1.0 + nrm(ks[10], (DEPTH, 2, ATT_HEAD_DIM), 0.05),
        'ml_head_g': 1.0 + nrm(ks[11], (DEPTH, ML_V_W), 0.05),
        'pool_w': nrm(ks[12], (DEPTH, len(POOL_WINDOWS), POOL_GROUP, POOL_GROUP), POOL_GROUP ** -0.5),
        'pool_scale': 1.0 + nrm(ks[13], (DEPTH, POOL_W), 0.1),
        'w_up_att': nrm(ks[14], (DEPTH, ATT_Q_W, D), ATT_Q_W ** -0.5),
        'w_up_ml': nrm(ks[15], (DEPTH, ML_V_W, D), ML_V_W ** -0.5),
        'w_up_pool': nrm(ks[16], (DEPTH, POOL_W, D), POOL_W ** -0.5),
        'w_out': nrm(ks[17], (DEPTH, D, D), D ** -0.5),
        'w_ffn_in': nrm(ks[18], (DEPTH, D, 2 * D_FF), D ** -0.5),
        'w_ffn_out': nrm(ks[19], (DEPTH, D_FF, D), D_FF ** -0.5),
    }


def reference(x, c, ctx, c_ctx, w_mod, b_mod, norm_g, w_in, ml_gate_b, qk_norm_g, ml_head_g,
              pool_w, pool_scale, w_up_att, w_up_ml, w_up_pool, w_out, w_ffn_in, w_ffn_out):
    cos, sin = _axial_rope_tables(x.shape[1])
    xc = ctx
    silu_c = jax.nn.silu(c)
    silu_cc = jax.nn.silu(c_ctx)
    for l in range(DEPTH):
        last = l == DEPTH - 1
        mod = (silu_c @ w_mod[l] + b_mod[l])[:, None]
        mod_c = (silu_cc @ w_mod[l] + b_mod[l])[None, None]
        sh_m, sc_m, gt_m, sh_f, sc_f, gt_f = jnp.split(mod, 6, axis=-1)
        csh_m, csc_m, cgt_m, csh_f, csc_f, cgt_f = jnp.split(mod_c, 6, axis=-1)

        h = _modulate(_rmsnorm(x, norm_g[l, 0]), sh_m, sc_m)
        hc = _modulate(_rmsnorm(xc, norm_g[l, 0]), csh_m, csc_m)
        y, yc = _token_mixers(h, hc, cos, sin, w_in[l], ml_gate_b[l], qk_norm_g[l], ml_head_g[l],
                              pool_w[l], pool_scale[l], w_up_att[l], w_up_ml[l], w_up_pool[l],
                              w_out[l], not last)
        x = x + gt_m * _rmsnorm(y, norm_g[l, 1])
        h = _modulate(_rmsnorm(x, norm_g[l, 2]), sh_f, sc_f)
        x = x + gt_f * _rmsnorm(_swiglu(h, w_ffn_in[l], w_ffn_out[l]), norm_g[l, 3])
        if not last:
            xc = xc + cgt_m * _rmsnorm(yc, norm_g[l, 1])
            hc = _modulate(_rmsnorm(xc, norm_g[l, 2]), csh_f, csc_f)
            xc = xc + cgt_f * _rmsnorm(_swiglu(hc, w_ffn_in[l], w_ffn_out[l]), norm_g[l, 3])
    return x
```

```python
import functools
import math

import jax
import jax.numpy as jnp
from jax import lax
from jax.experimental import pallas as pl
from jax.experimental.pallas import tpu as pltpu

F32 = jnp.float32
BF16 = jnp.bfloat16

D = 2048
B = 4
S = 4096
SC = 256
DEPTH = 2
GRID_W = 64
ATT_H = 8
ATT_KVH = 2
ATT_G = ATT_H // ATT_KVH
HD = 128
ROPE_THETA = 10000.0
ML_H = 4
ML_DK = 128
ML_DV = 256
ML_L = 128
POOL_WINDOWS = (2, 4, 8, 16)
POOL_G = 256
D_FF = 5632
EPS = 1e-6

N_LAT = B * S
N_CTX = B * SC
N_TOK = N_LAT + N_CTX

ATT_QW = ATT_H * HD
ATT_KVW = ATT_KVH * HD
ML_QKW = ML_H * ML_DK
ML_VW = ML_H * ML_DV
POOL_W = len(POOL_WINDOWS) * POOL_G

_P_GATE = 0
_P_O = 3 * D
_P_POOL = _P_O + ML_VW
_P_MLV = _P_POOL + POOL_W
_P_Q = _P_MLV + ML_VW
_P_MLK = _P_Q + ATT_QW
_P_MLQ = _P_MLK + ML_QKW
_P_K = _P_MLQ + ML_QKW
_P_V = _P_K + ATT_KVW
_P_IF = _P_V + ATT_KVW
_P_W = 12288
LANES = 128

VMEM_LIMIT = 56 * 1024 * 1024


def _cparams(sem, vmem=VMEM_LIMIT):
    return pltpu.CompilerParams(dimension_semantics=sem, vmem_limit_bytes=vmem)


def _mod_kernel(c_ref, w_ref, b_ref, o_ref):
    c = c_ref[...]
    sc = (c * jax.nn.sigmoid(c)).astype(BF16)
    o_ref[...] = jnp.dot(sc, w_ref[...].astype(BF16), preferred_element_type=F32) + b_ref[...]


def _mod_call(c8, w_mod, b_mod):
    tn = 1024
    n = w_mod.shape[1]
    return pl.pallas_call(
        _mod_kernel,
        out_shape=jax.ShapeDtypeStruct((8, n), F32),
        grid=(n // tn,),
        in_specs=[pl.BlockSpec((8, D), lambda j: (0, 0)),
                  pl.BlockSpec((D, tn), lambda j: (0, j)),
                  pl.BlockSpec((1, tn), lambda j: (0, j))],
        out_specs=pl.BlockSpec((8, tn), lambda j: (0, j)),
        compiler_params=_cparams(("parallel",)),
        name="mod_proj",
    )(c8, w_mod, b_mod.reshape(1, n))


_ROW_CHUNK = 64


def _norm_modulate_rows(x_ref, g_ref, sh_ref, sc_ref, h_ref, tm):
    g = g_ref[...]
    one_plus = 1.0 + sc_ref[...]
    sh = sh_ref[...]

    def body(r, carry):
        rows = pl.ds(pl.multiple_of(r * _ROW_CHUNK, _ROW_CHUNK), _ROW_CHUNK)
        xf = x_ref[rows, :]
        ms = jnp.mean(xf * xf, axis=-1, keepdims=True)
        y = xf * lax.rsqrt(ms + EPS) * g
        h_ref[rows, :] = (y * one_plus + sh).astype(BF16)
        return carry

    lax.fori_loop(0, tm // _ROW_CHUNK, body, 0)


def _proj_kernel(x_ref, g_ref, sh_ref, sc_ref, w_ref, o_ref, h_scr, *, tm, swiglu):
    @pl.when(pl.program_id(1) == 0)
    def _():
        _norm_modulate_rows(x_ref, g_ref, sh_ref, sc_ref, h_scr, tm)

    acc = jnp.dot(h_scr[...], w_ref[...], preferred_element_type=F32)
    if swiglu:
        half = acc.shape[1] // 2
        a = acc[:, :half]
        o_ref[...] = (a * jax.nn.sigmoid(a) * acc[:, half:]).astype(o_ref.dtype)
    else:
        o_ref[...] = acc.astype(o_ref.dtype)


def _proj_call(x, g, shift, scale, w, *, rows, tm, tn, swiglu, name):
    n = w.shape[1]
    tn_out = tn // 2 if swiglu else tn
    n_out = n // 2 if swiglu else n
    return pl.pallas_call(
        functools.partial(_proj_kernel, tm=tm, swiglu=swiglu),
        out_shape=jax.ShapeDtypeStruct((rows, n_out), BF16),
        grid=(rows // tm, n // tn),
        in_specs=[pl.BlockSpec((tm, D), lambda i, j: (i, 0)),
                  pl.BlockSpec((1, D), lambda i, j: (0, 0)),
                  pl.BlockSpec((None, 1, D), lambda i, j: ((i * tm) // S, 0, 0)),
                  pl.BlockSpec((None, 1, D), lambda i, j: ((i * tm) // S, 0, 0)),
                  pl.BlockSpec((D, tn), lambda i, j: (0, j))],
        out_specs=pl.BlockSpec((tm, tn_out), lambda i, j: (i, j)),
        scratch_shapes=[pltpu.VMEM((tm, D), BF16)],
        compiler_params=_cparams(("parallel", "arbitrary")),
        name=name,
    )(x, g, shift, scale, w)


def _resid_kernel(a_ref, w_ref, x_ref, g_ref, gate_ref, o_ref):
    z = jnp.dot(a_ref[...], w_ref[...], preferred_element_type=F32)
    ms = jnp.mean(z * z, axis=-1, keepdims=True)
    o_ref[...] = x_ref[...] + gate_ref[...] * (z * lax.rsqrt(ms + EPS) * g_ref[...])


def _resid_call(a, w, x, g, gate, *, rows, tm, name):
    k = w.shape[0]
    return pl.pallas_call(
        _resid_kernel,
        out_shape=jax.ShapeDtypeStruct((rows, D), F32),
        grid=(rows // tm,),
        in_specs=[pl.BlockSpec((tm, k), lambda i: (i, 0)),
                  pl.BlockSpec((k, D), lambda i: (0, 0), pipeline_mode=pl.Buffered(1)),
                  pl.BlockSpec((tm, D), lambda i: (i, 0)),
                  pl.BlockSpec((1, D), lambda i: (0, 0)),
                  pl.BlockSpec((None, 1, D), lambda i: ((i * tm) // S, 0, 0))],
        out_specs=pl.BlockSpec((tm, D), lambda i: (i, 0)),
        compiler_params=_cparams(("parallel",)),
        name=name,
    )(a, w, x, g, gate)


def _head_norm_rope(xh, g, cos, sin, first_half):
    xf = xh.astype(F32)
    ms = jnp.mean(xf * xf, axis=-1, keepdims=True)
    y = xf * lax.rsqrt(ms + EPS) * g
    swapped = jnp.where(first_half, pltpu.roll(y, HD - HD // 4, 1), pltpu.roll(y, HD // 4, 1))
    return y * cos + swapped * sin


def _qkprep_kernel(q_ref, k_ref, cos_ref, sin_ref, gq_ref, gk_ref, qo_ref, kt_ref, *, lat_tiles):
    use_rope = pl.program_id(0) < lat_tiles
    cos = jnp.where(use_rope, cos_ref[...], 1.0)
    sin = jnp.where(use_rope, sin_ref[...], 0.0)
    lane = lax.broadcasted_iota(jnp.int32, (1, HD), 1)
    first_half = (lane % (HD // 2)) < (HD // 4)
    q_scale = HD ** -0.5
    for h in range(ATT_H):
        cs = slice(h * HD, (h + 1) * HD)
        y = _head_norm_rope(q_ref[:, cs], gq_ref[...], cos, sin, first_half)
        qo_ref[:, cs] = (y * q_scale).astype(BF16)
    for h in range(ATT_KVH):
        cs = slice(h * HD, (h + 1) * HD)
        y = _head_norm_rope(k_ref[:, cs], gk_ref[...], cos, sin, first_half)
        kt_ref[cs, :] = y.T.astype(BF16)


def _qkprep_call(p, cos_t, sin_t, gq, gk):
    tm = 512
    lat_tiles = N_LAT // tm
    pos_tiles = S // tm
    return pl.pallas_call(
        functools.partial(_qkprep_kernel, lat_tiles=lat_tiles),
        out_shape=(jax.ShapeDtypeStruct((N_TOK, ATT_QW), BF16),
                   jax.ShapeDtypeStruct((ATT_KVW, N_TOK), BF16)),
        grid=(N_TOK // tm,),
        in_specs=[pl.BlockSpec((tm, ATT_QW), lambda i: (i, _P_Q // ATT_QW)),
                  pl.BlockSpec((tm, ATT_KVW), lambda i: (i, _P_K // ATT_KVW)),
                  pl.BlockSpec((tm, HD), lambda i: (i % pos_tiles, 0)),
                  pl.BlockSpec((tm, HD), lambda i: (i % pos_tiles, 0)),
                  pl.BlockSpec((1, HD), lambda i: (0, 0)),
                  pl.BlockSpec((1, HD), lambda i: (0, 0))],
        out_specs=(pl.BlockSpec((tm, ATT_QW), lambda i: (i, 0)),
                   pl.BlockSpec((ATT_KVW, tm), lambda i: (0, i))),
        compiler_params=_cparams(("parallel",)),
        name="qk_prep",
    )(p, p, cos_t, sin_t, gq, gk)


_TQ = 256


def _softmax_pv(q, kts, vs):
    scores = [jnp.dot(q, kt[...], preferred_element_type=F32) for kt in kts]
    m = functools.reduce(jnp.maximum, [jnp.max(s, axis=-1, keepdims=True) for s in scores])
    probs = [jnp.exp(s - m) for s in scores]
    denom = functools.reduce(lambda a, b: a + b, [jnp.sum(pr, axis=-1, keepdims=True) for pr in probs])
    out = functools.reduce(lambda a, b: a + b,
                           [jnp.dot(pr.astype(BF16), v[...], preferred_element_type=F32)
                            for pr, v in zip(probs, vs)])
    return out / denom


def _attn_kernel(q_ref, ktl_ref, ktc_ref, vl_ref, vc_ref, o_ref, *, lat_tiles, ctx_step):
    def latent():
        for h in range(ATT_G):
            cs = slice(h * HD, (h + 1) * HD)
            o_ref[:, cs] = _softmax_pv(q_ref[:, cs], (ktl_ref, ktc_ref), (vl_ref, vc_ref)).astype(BF16)

    def context():
        for h in range(ATT_G):
            cs = slice(h * HD, (h + 1) * HD)
            o_ref[:, cs] = _softmax_pv(q_ref[:, cs], (ktc_ref,), (vc_ref,)).astype(BF16)

    if ctx_step:
        t = pl.program_id(2)
        pl.when(t < lat_tiles)(latent)
        pl.when(t == lat_tiles)(context)
    else:
        latent()


def _attn_call(q_proc, kt, p, *, ctx_step):
    lat_tiles = S // _TQ
    ctx_blk = N_LAT // SC
    steps = lat_tiles + (1 if ctx_step else 0)
    rows = N_TOK if ctx_step else N_LAT
    gw = ATT_G * HD

    def q_map(b, kv, t):
        return (jnp.where(t < lat_tiles, b * lat_tiles + t, ctx_blk + b), kv)

    return pl.pallas_call(
        functools.partial(_attn_kernel, lat_tiles=lat_tiles, ctx_step=ctx_step),
        out_shape=jax.ShapeDtypeStruct((rows, ATT_QW), BF16),
        grid=(B, ATT_KVH, steps),
        in_specs=[pl.BlockSpec((_TQ, gw), q_map),
                  pl.BlockSpec((HD, S), lambda b, kv, t: (kv, b)),
                  pl.BlockSpec((HD, SC), lambda b, kv, t: (kv, ctx_blk + b)),
                  pl.BlockSpec((S, HD), lambda b, kv, t: (b, _P_V // HD + kv)),
                  pl.BlockSpec((SC, HD), lambda b, kv, t: (ctx_blk + b, _P_V // HD + kv))],
        out_specs=pl.BlockSpec((_TQ, gw), q_map),
        compiler_params=_cparams(("parallel", "parallel", "arbitrary")),
        name="gqa_attention",
    )(q_proc, kt, kt, p, p)


_ML_CHUNKS = (SC + S) // ML_L
_ML_CTX_CHUNKS = SC // ML_L
_NEG = -1e30


def _log_sigmoid(x):
    return jnp.minimum(x, 0.0) - jnp.log1p(jnp.exp(-jnp.abs(x)))


def _exact_dot_f32(a01, x):
    hi = x.astype(BF16)
    r1 = x - hi.astype(F32)
    mid = r1.astype(BF16)
    lo = (r1 - mid.astype(F32)).astype(BF16)
    return (jnp.dot(a01, hi, preferred_element_type=F32)
            + jnp.dot(a01, mid, preferred_element_type=F32)
            + jnp.dot(a01, lo, preferred_element_type=F32))


def _mlstm_kernel(qf_ref, kf_ref, vf_ref, gf_ref, qb_ref, kb_ref, vb_ref, gb_ref, bias_ref,
                  hf_ref, hb_ref, ct_scr, n_scr, m_scr):
    @pl.when(pl.program_id(1) == 0)
    def _():
        ct_scr[...] = jnp.zeros_like(ct_scr)
        n_scr[...] = jnp.zeros_like(n_scr)
        m_scr[...] = jnp.zeros_like(m_scr)

    L = ML_L
    row = lax.broadcasted_iota(jnp.int32, (L, L), 0)
    col = lax.broadcasted_iota(jnp.int32, (L, L), 1)
    q_scale = ML_DK ** -0.5

    for d, (q_ref, k_ref, v_ref, g_ref, h_ref) in enumerate(
            ((qf_ref, kf_ref, vf_ref, gf_ref, hf_ref), (qb_ref, kb_ref, vb_ref, gb_ref, hb_ref))):
        mask = (col <= row) if d == 0 else (col >= row)
        tri = jnp.where(mask, 1.0, 0.0).astype(BF16)
        gates = g_ref[...].astype(F32) + bias_ref[...]
        log_f = _log_sigmoid(gates)
        cum_lf = _exact_dot_f32(tri, log_f)
        gates_t = gates.T
        cum_lf_t = cum_lf.T
        tot_lf = jnp.sum(log_f, axis=0, keepdims=True)
        for h in range(ML_H):
            idx = d * ML_H + h
            li_lane = _gate_lane(0, d, h)
            lf_lane = _gate_lane(1, d, h)
            b_c = cum_lf[:, lf_lane:lf_lane + 1]
            b_r = cum_lf_t[lf_lane:lf_lane + 1, :]
            li_c = gates[:, li_lane:li_lane + 1]
            li_r = gates_t[li_lane:li_lane + 1, :]
            g_tot = tot_lf[:, lf_lane:lf_lane + 1]
            m_prev = m_scr[idx][0:1, 0:1]
            n_prev = n_scr[idx][0:1, :]
            ct_prev = ct_scr[idx]

            q = (q_ref[:, h * ML_DK:(h + 1) * ML_DK].astype(F32) * q_scale).astype(BF16)
            k = k_ref[:, h * ML_DK:(h + 1) * ML_DK]
            v = v_ref[:, h * ML_DV:(h + 1) * ML_DV]

            qk = lax.dot_general(q, k, (((1,), (1,)), ((), ())), preferred_element_type=F32)
            d_log = jnp.where(mask, b_c - b_r + li_r, _NEG)
            a_log = b_c + m_prev
            m_j = jnp.maximum(a_log, jnp.max(d_log, axis=-1, keepdims=True))
            w = jnp.exp(d_log - m_j) * qk
            inter = jnp.exp(a_log - m_j)
            num = (jnp.dot(w.astype(BF16), v, preferred_element_type=F32)
                   + inter * jnp.dot(q, ct_prev.astype(BF16), preferred_element_type=F32))
            nq = jnp.sum(q.astype(F32) * n_prev, axis=-1, keepdims=True)
            den = jnp.sum(w, axis=-1, keepdims=True) + inter * nq
            h_out = num / jnp.maximum(jnp.abs(den), jnp.exp(-m_j))
            h_ref[:, h * ML_DV:(h + 1) * ML_DV] = h_out.astype(h_ref.dtype)

            w_end = g_tot - b_c + li_c
            m_loc = jnp.max(w_end, axis=0, keepdims=True)
            e_c = jnp.exp(w_end - m_loc)
            m_new = jnp.maximum(g_tot + m_prev, m_loc)
            a_old = jnp.exp(g_tot + m_prev - m_new)
            a_loc = jnp.exp(m_loc - m_new)
            ev = (e_c * v.astype(F32)).astype(BF16)
            c_loc = lax.dot_general(k, ev, (((0,), (0,)), ((), ())), preferred_element_type=F32)
            n_loc = jnp.sum(e_c * k.astype(F32), axis=0, keepdims=True)
            ct_scr[idx] = a_old * ct_prev + a_loc * c_loc
            n_scr[idx] = jnp.broadcast_to(a_old * n_prev + a_loc * n_loc, (8, ML_DK))
            m_scr[idx] = jnp.broadcast_to(m_new, (8, LANES))


def _gate_lane(kind, d, h):
    return kind * 2 * ML_H + d * ML_H + h


def _mlstm_call(p, gate_bias):
    lat_chunks = S // ML_L
    ctx_blk = N_LAT // ML_L

    def fwd_rb(b, t):
        return jnp.where(t < _ML_CTX_CHUNKS, ctx_blk + b * _ML_CTX_CHUNKS + t,
                         b * lat_chunks + t - _ML_CTX_CHUNKS)

    def rev_rb(b, t):
        return jnp.where(t < _ML_CTX_CHUNKS, ctx_blk + b * _ML_CTX_CHUNKS + (_ML_CTX_CHUNKS - 1 - t),
                         b * lat_chunks + (_ML_CHUNKS - 1 - t))

    def specs(rb):
        return [pl.BlockSpec((ML_L, ML_QKW), lambda b, t: (rb(b, t), _P_MLQ // ML_QKW)),
                pl.BlockSpec((ML_L, ML_QKW), lambda b, t: (rb(b, t), _P_MLK // ML_QKW)),
                pl.BlockSpec((ML_L, ML_VW), lambda b, t: (rb(b, t), _P_MLV // ML_VW)),
                pl.BlockSpec((ML_L, LANES), lambda b, t: (rb(b, t), _P_IF // LANES))]

    n_chain = 2 * ML_H
    return pl.pallas_call(
        _mlstm_kernel,
        out_shape=(jax.ShapeDtypeStruct((N_TOK, ML_VW), BF16),
                   jax.ShapeDtypeStruct((N_TOK, ML_VW), BF16)),
        grid=(B, _ML_CHUNKS),
        in_specs=specs(fwd_rb) + specs(rev_rb) + [pl.BlockSpec((1, LANES), lambda b, t: (0, 0))],
        out_specs=(pl.BlockSpec((ML_L, ML_VW), lambda b, t: (fwd_rb(b, t), 0)),
                   pl.BlockSpec((ML_L, ML_VW), lambda b, t: (rev_rb(b, t), 0))),
        scratch_shapes=[pltpu.VMEM((n_chain, ML_DK, ML_DV), F32),
                        pltpu.VMEM((n_chain, 8, ML_DK), F32),
                        pltpu.VMEM((n_chain, 8, LANES), F32)],
        compiler_params=_cparams(("parallel", "arbitrary")),
        name="mlstm",
    )(p, p, p, p, p, p, p, p, gate_bias)


def _pool_kernel(u_ref, w_ref, sc_ref, o_ref, mixed_scr, *, seq):
    gi = pl.program_id(1)
    t = lax.broadcasted_iota(jnp.int32, (seq, 1), 0)

    def prev(x, k):
        return jnp.where(t >= k, pltpu.roll(x, k, 0), 0.0)

    def nxt(x, k):
        return jnp.where(t < seq - k, pltpu.roll(x, seq - k, 0), 0.0)

    half_w = jnp.where(gi == 0, 1, jnp.where(gi == 1, 2, jnp.where(gi == 2, 4, 8)))
    count = (jnp.minimum(t + half_w, seq) - jnp.maximum(t - half_w, 0)).astype(F32)
    for c in range(POOL_G // LANES):
        cs = slice(c * LANES, (c + 1) * LANES)
        u = u_ref[:, cs].astype(F32)
        before = prev(u, 1)
        after = u
        win = before + after
        for step, k in enumerate((1, 2, 4)):
            before = before + prev(before, k)
            after = after + nxt(after, k)
            win = jnp.where(gi == step + 1, before + after, win)
        mixed_scr[:, cs] = (win / count - u).astype(BF16)
    o_ref[...] = (jnp.dot(mixed_scr[...], w_ref[...], preferred_element_type=F32)
                  * sc_ref[...]).astype(o_ref.dtype)


def _pool_call(p, pool_w, pool_scale, prev_out, *, seq, first_blk):
    n_groups = len(POOL_WINDOWS)
    in_specs = [pl.BlockSpec((seq, POOL_G), lambda b, g: (first_blk + b, _P_POOL // POOL_G + g)),
                pl.BlockSpec((None, POOL_G, POOL_G), lambda b, g: (g, 0, 0)),
                pl.BlockSpec((1, POOL_G), lambda b, g: (0, g))]
    args = [p, pool_w, pool_scale]
    aliases = {}
    if prev_out is not None:
        in_specs.append(pl.BlockSpec(memory_space=pl.ANY))
        args.append(prev_out)
        aliases = {3: 0}

    def kern(u_ref, w_ref, sc_ref, *rest):
        o_ref, mixed_scr = rest[-2], rest[-1]
        _pool_kernel(u_ref, w_ref, sc_ref, o_ref, mixed_scr, seq=seq)

    return pl.pallas_call(
        kern,
        out_shape=jax.ShapeDtypeStruct((N_TOK, POOL_W), BF16),
        grid=(B, n_groups),
        in_specs=in_specs,
        out_specs=pl.BlockSpec((seq, POOL_G), lambda b, g: (first_blk + b, g)),
        scratch_shapes=[pltpu.VMEM((seq, POOL_G), BF16)],
        input_output_aliases=aliases,
        compiler_params=_cparams(("parallel", "parallel")),
        name="pool_branch_%d" % seq,
    )(*args)


def _merge_kernel(att_ref, hf_ref, hb_ref, og_ref, pool_ref, g0_ref, g1_ref, g2_ref,
                  wa_ref, wm_ref, wp_ref, mlg_ref, y_ref, ml_scr):
    @pl.when(pl.program_id(1) == 0)
    def _():
        for h in range(ML_H):
            cs = slice(h * ML_DV, (h + 1) * ML_DV)
            hs = hf_ref[:, cs].astype(F32) + hb_ref[:, cs].astype(F32)
            ms = jnp.mean(hs * hs, axis=-1, keepdims=True)
            normed = hs * lax.rsqrt(ms + EPS) * mlg_ref[:, cs]
            ml_scr[:, cs] = (normed * jax.nn.sigmoid(og_ref[:, cs].astype(F32))).astype(BF16)

    def branch(g_ref, a, w_ref):
        return jax.nn.sigmoid(g_ref[...].astype(F32)) * jnp.dot(a, w_ref[...], preferred_element_type=F32)

    y = (branch(g0_ref, att_ref[...], wa_ref) + branch(g1_ref, ml_scr[...], wm_ref)
         + branch(g2_ref, pool_ref[...], wp_ref))
    y_ref[...] = y.astype(y_ref.dtype)


def _merge_call(att, hf, hb, p, pool, w_up_att, w_up_ml, w_up_pool, ml_head_g, *, rows):
    tm, tn = 512, 1024
    nj = D // tn

    def row_blk(width):
        return pl.BlockSpec((tm, width), lambda i, j: (i, 0))

    def gate_blk(branch):
        return pl.BlockSpec((tm, tn), lambda i, j: (i, (_P_GATE + branch * D) // tn + j))

    def w_blk(k):
        return pl.BlockSpec((k, tn), lambda i, j: (0, j))

    return pl.pallas_call(
        _merge_kernel,
        out_shape=jax.ShapeDtypeStruct((rows, D), BF16),
        grid=(rows // tm, nj),
        in_specs=[row_blk(ATT_QW), row_blk(ML_VW), row_blk(ML_VW),
                  pl.BlockSpec((tm, ML_VW), lambda i, j: (i, _P_O // ML_VW)),
                  row_blk(POOL_W), gate_blk(0), gate_blk(1), gate_blk(2),
                  w_blk(ATT_QW), w_blk(ML_VW), w_blk(POOL_W),
                  pl.BlockSpec((1, ML_VW), lambda i, j: (0, 0))],
        out_specs=pl.BlockSpec((tm, tn), lambda i, j: (i, j)),
        scratch_shapes=[pltpu.VMEM((tm, ML_VW), BF16)],
        compiler_params=_cparams(("parallel", "arbitrary")),
        name="branch_merge",
    )(att, hf, hb, p, pool, p, p, p, w_up_att, w_up_ml, w_up_pool, ml_head_g)


def _regroup_w_in(w):
    kv, mk, mv, q = ATT_KVW, ML_QKW, ML_VW, ATT_QW
    o = 0
    k_w = w[:, o:o + kv]; o += kv
    v_w = w[:, o:o + kv]; o += kv
    mlk_w = w[:, o:o + mk]; o += mk
    mlv_w = w[:, o:o + mv]; o += mv
    if_w = w[:, o:o + 4 * ML_H]; o += 4 * ML_H
    q_w = w[:, o:o + q]; o += q
    mlq_w = w[:, o:o + mk]; o += mk
    og_w = w[:, o:o + mv]; o += mv
    pool_w = w[:, o:o + POOL_W]; o += POOL_W
    gate_w = w[:, o:o + 3 * D]; o += 3 * D
    pad = jnp.zeros((D, _P_W - _P_IF - 4 * ML_H), w.dtype)
    return jnp.concatenate([gate_w, og_w, pool_w, mlv_w, q_w, mlk_w, mlq_w, k_w, v_w, if_w, pad],
                           axis=1).astype(BF16)


def _interleave_ffn_w(w, tn):
    half = tn // 2
    gt = w[:, :D_FF].reshape(D, D_FF // half, half)
    up = w[:, D_FF:].reshape(D, D_FF // half, half)
    return jnp.concatenate([gt, up], axis=2).reshape(D, 2 * D_FF).astype(BF16)


def _rope_tables():
    t = jnp.arange(S)
    row = (t // GRID_W).astype(F32)
    colp = (t % GRID_W).astype(F32)
    n_freq = HD // 4
    inv = ROPE_THETA ** (-jnp.arange(n_freq, dtype=F32) / n_freq)
    ar, ac = row[:, None] * inv, colp[:, None] * inv
    cos_t = jnp.concatenate([jnp.cos(ar), jnp.cos(ar), jnp.cos(ac), jnp.cos(ac)], axis=1)
    sin_t = jnp.concatenate([-jnp.sin(ar), jnp.sin(ar), -jnp.sin(ac), jnp.sin(ac)], axis=1)
    return cos_t, sin_t


_FFN_TN = 1024


def kernel(x, c, ctx, c_ctx, w_mod, b_mod, norm_g, w_in, ml_gate_b, qk_norm_g, ml_head_g,
           pool_w, pool_scale, w_up_att, w_up_ml, w_up_pool, w_out, w_ffn_in, w_ffn_out):
    cos_t, sin_t = _rope_tables()
    xa = jnp.concatenate([x.reshape(N_LAT, D), ctx.reshape(N_CTX, D)], axis=0)
    c8 = jnp.concatenate([c, c_ctx[None], jnp.zeros((8 - B - 1, D), F32)], axis=0)

    for l in range(DEPTH):
        last = l == DEPTH - 1
        rows = N_LAT if last else N_TOK

        mod = _mod_call(c8, w_mod[l], b_mod[l]).reshape(8, 6, 1, D)
        sh_m, sc_m, gt_m, sh_f, sc_f, gt_f = (mod[:, i] for i in range(6))

        p = _proj_call(xa, norm_g[l, 0][None], sh_m, sc_m, _regroup_w_in(w_in[l]),
                       rows=N_TOK, tm=1024, tn=1024, swiglu=False, name="in_proj")

        q_proc, kt = _qkprep_call(p, cos_t, sin_t, qk_norm_g[l, 0][None], qk_norm_g[l, 1][None])
        att = _attn_call(q_proc, kt, p, ctx_step=not last)

        gb = jnp.transpose(ml_gate_b[l], (1, 0, 2)).reshape(1, 4 * ML_H)
        gb = jnp.pad(gb, ((0, 0), (0, LANES - 4 * ML_H)))
        hf, hb = _mlstm_call(p, gb)

        pool = _pool_call(p, pool_w[l].astype(BF16), pool_scale[l][None], None, seq=S, first_blk=0)
        if not last:
            pool = _pool_call(p, pool_w[l].astype(BF16), pool_scale[l][None], pool,
                              seq=SC, first_blk=N_LAT // SC)

        y = _merge_call(att, hf, hb, p, pool, w_up_att[l].astype(BF16), w_up_ml[l].astype(BF16),
                        w_up_pool[l].astype(BF16), ml_head_g[l][None], rows=rows)
        xa = _resid_call(y, w_out[l].astype(BF16), xa, norm_g[l, 1][None], gt_m,
                         rows=rows, tm=512, name="out_proj")

        f = _proj_call(xa, norm_g[l, 2][None], sh_f, sc_f, _interleave_ffn_w(w_ffn_in[l], _FFN_TN),
                       rows=rows, tm=1024, tn=_FFN_TN, swiglu=True, name="ffn_in")
        xa = _resid_call(f, w_ffn_out[l].astype(BF16), xa, norm_g[l, 3][None], gt_f,
                         rows=rows, tm=256, name="ffn_out")

    return xa.reshape(B, S, D)
```

```python
import functools
import math

import jax
import jax.numpy as jnp
from jax import lax
from jax.experimental import pallas as pl
from jax.experimental.pallas import tpu as pltpu

F32 = jnp.float32
BF16 = jnp.bfloat16

D = 2048
B = 4
S = 4096
SC = 256
DEPTH = 2
GRID_W = 64
ATT_H = 8
ATT_KVH = 2
ATT_G = ATT_H // ATT_KVH
HD = 128
ROPE_THETA = 10000.0
ML_H = 4
ML_DK = 128
ML_DV = 256
ML_L = 128
POOL_WINDOWS = (2, 4, 8, 16)
POOL_G = 256
D_FF = 5632
EPS = 1e-6

N_LAT = B * S
N_CTX = B * SC
N_TOK = N_LAT + N_CTX

ATT_QW = ATT_H * HD
ATT_KVW = ATT_KVH * HD
ML_QKW = ML_H * ML_DK
ML_VW = ML_H * ML_DV
POOL_W = len(POOL_WINDOWS) * POOL_G

_P_GATE = 0
_P_O = 3 * D
_P_POOL = _P_O + ML_VW
_P_MLV = _P_POOL + POOL_W
_P_Q = _P_MLV + ML_VW
_P_MLK = _P_Q + ATT_QW
_P_MLQ = _P_MLK + ML_QKW
_P_K = _P_MLQ + ML_QKW
_P_V = _P_K + ATT_KVW
_P_IF = _P_V + ATT_KVW
_P_W = 12288
_P_TN = 1024
_P_SIGMOID_TILES = (_P_O + ML_VW) // _P_TN
LANES = 128

VMEM_LIMIT = 56 * 1024 * 1024


def _cparams(sem, vmem=VMEM_LIMIT):
    return pltpu.CompilerParams(dimension_semantics=sem, vmem_limit_bytes=vmem)


def _sigmoid(x):
    return 0.5 * jnp.tanh(0.5 * x) + 0.5


def _mod_kernel(c_ref, w_ref, b_ref, o_ref):
    c = c_ref[...]
    sc = (c * _sigmoid(c)).astype(BF16)
    o_ref[...] = jnp.dot(sc, w_ref[...].astype(BF16), preferred_element_type=F32) + b_ref[...]


def _mod_call(c8, w_mod, b_mod, layer):
    tn = 1024
    n = w_mod.shape[2]
    return pl.pallas_call(
        _mod_kernel,
        out_shape=jax.ShapeDtypeStruct((8, n), F32),
        grid=(n // tn,),
        in_specs=[pl.BlockSpec((8, D), lambda j: (0, 0)),
                  pl.BlockSpec((None, D, tn), lambda j: (layer, 0, j)),
                  pl.BlockSpec((None, 1, tn), lambda j: (layer, 0, j))],
        out_specs=pl.BlockSpec((8, tn), lambda j: (0, j)),
        compiler_params=_cparams(("parallel",)),
        name="mod_proj",
    )(c8, w_mod, b_mod.reshape(DEPTH, 1, n))


_ROW_CHUNK = 64


def _norm_modulate_rows(x_ref, g_ref, sh_ref, sc_ref, h_ref, tm):
    g = g_ref[...]
    one_plus = 1.0 + sc_ref[...]
    sh = sh_ref[...]

    def body(r, carry):
        rows = pl.ds(pl.multiple_of(r * _ROW_CHUNK, _ROW_CHUNK), _ROW_CHUNK)
        xf = x_ref[rows, :]
        ms = jnp.mean(xf * xf, axis=-1, keepdims=True)
        y = xf * lax.rsqrt(ms + EPS) * g
        h_ref[rows, :] = (y * one_plus + sh).astype(BF16)
        return carry

    lax.fori_loop(0, tm // _ROW_CHUNK, body, 0)


def _in_proj_kernel(x_ref, g_ref, sh_ref, sc_ref, w_ref, o_ref, h_scr, *, tm):
    j = pl.program_id(1)

    @pl.when(j == 0)
    def _():
        _norm_modulate_rows(x_ref, g_ref, sh_ref, sc_ref, h_scr, tm)

    acc = jnp.dot(h_scr[...], w_ref[...], preferred_element_type=F32)

    @pl.when(j < _P_SIGMOID_TILES)
    def _():
        o_ref[...] = _sigmoid(acc).astype(o_ref.dtype)

    @pl.when(j >= _P_SIGMOID_TILES)
    def _():
        o_ref[...] = acc.astype(o_ref.dtype)


def _ffn_in_kernel(x_ref, g_ref, sh_ref, sc_ref, wg_ref, wu_ref, o_ref, h_scr, *, tm):
    @pl.when(pl.program_id(1) == 0)
    def _():
        _norm_modulate_rows(x_ref, g_ref, sh_ref, sc_ref, h_scr, tm)

    h = h_scr[...]
    a = jnp.dot(h, wg_ref[...].astype(BF16), preferred_element_type=F32)
    u = jnp.dot(h, wu_ref[...].astype(BF16), preferred_element_type=F32)
    o_ref[...] = (a * _sigmoid(a) * u).astype(o_ref.dtype)


def _row_specs(tm):
    return [pl.BlockSpec((tm, D), lambda i, j: (i, 0)),
            pl.BlockSpec((1, D), lambda i, j: (0, 0)),
            pl.BlockSpec((None, 1, D), lambda i, j: ((i * tm) // S, 0, 0)),
            pl.BlockSpec((None, 1, D), lambda i, j: ((i * tm) // S, 0, 0))]


def _in_proj_call(x, g, shift, scale, w, layer):
    tm, tn = 1024, _P_TN
    return pl.pallas_call(
        functools.partial(_in_proj_kernel, tm=tm),
        out_shape=jax.ShapeDtypeStruct((N_TOK, _P_W), BF16),
        grid=(N_TOK // tm, _P_W // tn),
        in_specs=_row_specs(tm) + [pl.BlockSpec((None, D, tn), lambda i, j: (layer, 0, j))],
        out_specs=pl.BlockSpec((tm, tn), lambda i, j: (i, j)),
        scratch_shapes=[pltpu.VMEM((tm, D), BF16)],
        compiler_params=_cparams(("parallel", "arbitrary")),
        name="in_proj",
    )(x, g, shift, scale, w)


def _ffn_in_call(x, g, shift, scale, w, layer, *, rows):
    tm, tn = 1024, 512
    nj = D_FF // tn
    return pl.pallas_call(
        functools.partial(_ffn_in_kernel, tm=tm),
        out_shape=jax.ShapeDtypeStruct((rows, D_FF), BF16),
        grid=(rows // tm, nj),
        in_specs=_row_specs(tm) + [pl.BlockSpec((None, D, tn), lambda i, j: (layer, 0, j)),
                                   pl.BlockSpec((None, D, tn), lambda i, j: (layer, 0, nj + j))],
        out_specs=pl.BlockSpec((tm, tn), lambda i, j: (i, j)),
        scratch_shapes=[pltpu.VMEM((tm, D), BF16)],
        compiler_params=_cparams(("parallel", "arbitrary")),
        name="ffn_in",
    )(x, g, shift, scale, w, w)


def _resid_kernel(a_ref, w_ref, x_ref, g_ref, gate_ref, o_ref):
    z = jnp.dot(a_ref[...], w_ref[...], preferred_element_type=F32)
    ms = jnp.mean(z * z, axis=-1, keepdims=True)
    o_ref[...] = x_ref[...] + gate_ref[...] * (z * lax.rsqrt(ms + EPS) * g_ref[...])


def _resid_call(a, w, x, g, gate, layer, *, rows, tm, name):
    k = w.shape[1]
    return pl.pallas_call(
        _resid_kernel,
        out_shape=jax.ShapeDtypeStruct((rows, D), F32),
        grid=(rows // tm,),
        in_specs=[pl.BlockSpec((tm, k), lambda i: (i, 0)),
                  pl.BlockSpec((None, k, D), lambda i: (layer, 0, 0), pipeline_mode=pl.Buffered(1)),
                  pl.BlockSpec((tm, D), lambda i: (i, 0)),
                  pl.BlockSpec((1, D), lambda i: (0, 0)),
                  pl.BlockSpec((None, 1, D), lambda i: ((i * tm) // S, 0, 0))],
        out_specs=pl.BlockSpec((tm, D), lambda i: (i, 0)),
        compiler_params=_cparams(("parallel",)),
        name=name,
    )(a, w, x, g, gate)


def _head_norm_rope(xh, g, cos, sin, first_half):
    xf = xh.astype(F32)
    ms = jnp.mean(xf * xf, axis=-1, keepdims=True)
    y = xf * lax.rsqrt(ms + EPS) * g
    swapped = jnp.where(first_half, pltpu.roll(y, HD - HD // 4, 1), pltpu.roll(y, HD // 4, 1))
    return y * cos + swapped * sin


def _qkprep_kernel(q_ref, k_ref, v_ref, cos_ref, sin_ref, gq_ref, gk_ref, qt_ref, ko_ref, vt_ref,
                   *, lat_tiles):
    use_rope = pl.program_id(0) < lat_tiles
    cos = jnp.where(use_rope, cos_ref[...], 1.0)
    sin = jnp.where(use_rope, sin_ref[...], 0.0)
    lane = lax.broadcasted_iota(jnp.int32, (1, HD), 1)
    first_half = (lane % (HD // 2)) < (HD // 4)
    q_scale = HD ** -0.5 * math.log2(math.e)
    for h in range(ATT_H):
        cs = slice(h * HD, (h + 1) * HD)
        y = _head_norm_rope(q_ref[:, cs], gq_ref[...], cos, sin, first_half)
        qt_ref[cs, :] = (y * q_scale).T.astype(BF16)
    for h in range(ATT_KVH):
        cs = slice(h * HD, (h + 1) * HD)
        y = _head_norm_rope(k_ref[:, cs], gk_ref[...], cos, sin, first_half)
        ko_ref[:, cs] = y.astype(BF16)
        vt_ref[cs, :] = v_ref[:, cs].astype(F32).T.astype(BF16)


def _qkprep_call(p, cos_t, sin_t, gq, gk):
    tm = 512
    lat_tiles = N_LAT // tm
    pos_tiles = S // tm
    return pl.pallas_call(
        functools.partial(_qkprep_kernel, lat_tiles=lat_tiles),
        out_shape=(jax.ShapeDtypeStruct((ATT_QW, N_TOK), BF16),
                   jax.ShapeDtypeStruct((N_TOK, ATT_KVW), BF16),
                   jax.ShapeDtypeStruct((ATT_KVW, N_TOK), BF16)),
        grid=(N_TOK // tm,),
        in_specs=[pl.BlockSpec((tm, ATT_QW), lambda i: (i, _P_Q // ATT_QW)),
                  pl.BlockSpec((tm, ATT_KVW), lambda i: (i, _P_K // ATT_KVW)),
                  pl.BlockSpec((tm, ATT_KVW), lambda i: (i, _P_V // ATT_KVW)),
                  pl.BlockSpec((tm, HD), lambda i: (i % pos_tiles, 0)),
                  pl.BlockSpec((tm, HD), lambda i: (i % pos_tiles, 0)),
                  pl.BlockSpec((1, HD), lambda i: (0, 0)),
                  pl.BlockSpec((1, HD), lambda i: (0, 0))],
        out_specs=(pl.BlockSpec((ATT_QW, tm), lambda i: (0, i)),
                   pl.BlockSpec((tm, ATT_KVW), lambda i: (i, 0)),
                   pl.BlockSpec((ATT_KVW, tm), lambda i: (0, i))),
        compiler_params=_cparams(("parallel",)),
        name="qk_prep",
    )(p, p, p, cos_t, sin_t, gq, gk)


_TQ = 256
_KC = 512


def _attn_kernel(qt_ref, kl_ref, kc_ref, vtl_ref, vtc_ref, o_ref, s_scr, p_scr, *, lat_tiles, ctx_step):
    def run(with_latent):
        chunks = []
        base = 0
        if with_latent:
            chunks += [(kl_ref, c * _KC, _KC, c * _KC) for c in range(S // _KC)]
            base = S
        chunks.append((kc_ref, 0, SC, base))

        def scores(h):
            slot = h % 2
            qt = qt_ref[h * HD:(h + 1) * HD, :]
            mx = None
            for k_ref, r0, nr, s0 in chunks:
                sc = jnp.dot(k_ref[r0:r0 + nr, :], qt, preferred_element_type=F32)
                s_scr[slot, s0:s0 + nr, :] = sc
                part = jnp.max(sc.reshape(nr // 8, 8, _TQ), axis=0)
                mx = part if mx is None else jnp.maximum(mx, part)
            return jnp.max(mx, axis=0, keepdims=True)

        def probs(h, m):
            slot = h % 2
            tot = None
            for _, _, nr, s0 in chunks:
                pr = jnp.exp2(s_scr[slot, s0:s0 + nr, :] - m)
                p_scr[slot, s0:s0 + nr, :] = pr.astype(BF16)
                part = jnp.sum(pr.reshape(nr // 8, 8, _TQ), axis=0)
                tot = part if tot is None else tot + part
            return jnp.sum(tot, axis=0, keepdims=True)

        def output(h, denom):
            slot = h % 2
            ot = jnp.dot(vtc_ref[...], p_scr[slot, base:base + SC, :], preferred_element_type=F32)
            if with_latent:
                ot = ot + jnp.dot(vtl_ref[...], p_scr[slot, 0:S, :], preferred_element_type=F32)
            o_ref[:, h * HD:(h + 1) * HD] = (ot * (1.0 / denom)).T.astype(o_ref.dtype)

        m = scores(0)
        for h in range(ATT_G):
            m_next = scores(h + 1) if h + 1 < ATT_G else None
            output(h, probs(h, m))
            m = m_next

    if ctx_step:
        t = pl.program_id(2)
        pl.when(t < lat_tiles)(functools.partial(run, True))
        pl.when(t == lat_tiles)(functools.partial(run, False))
    else:
        run(True)


def _attn_call(qt, k_proc, vt, *, ctx_step):
    lat_tiles = S // _TQ
    ctx_blk = N_LAT // SC
    steps = lat_tiles + (1 if ctx_step else 0)
    rows = N_TOK if ctx_step else N_LAT
    gw = ATT_G * HD

    def q_blk(b, t):
        return jnp.where(t < lat_tiles, b * lat_tiles + t, ctx_blk + b)

    return pl.pallas_call(
        functools.partial(_attn_kernel, lat_tiles=lat_tiles, ctx_step=ctx_step),
        out_shape=jax.ShapeDtypeStruct((rows, ATT_QW), BF16),
        grid=(B, ATT_KVH, steps),
        in_specs=[pl.BlockSpec((gw, _TQ), lambda b, kv, t: (kv, q_blk(b, t))),
                  pl.BlockSpec((S, HD), lambda b, kv, t: (b, kv)),
                  pl.BlockSpec((SC, HD), lambda b, kv, t: (ctx_blk + b, kv)),
                  pl.BlockSpec((HD, S), lambda b, kv, t: (kv, b)),
                  pl.BlockSpec((HD, SC), lambda b, kv, t: (kv, ctx_blk + b))],
        out_specs=pl.BlockSpec((_TQ, gw), lambda b, kv, t: (q_blk(b, t), kv)),
        scratch_shapes=[pltpu.VMEM((2, S + SC, _TQ), F32),
                        pltpu.VMEM((2, S + SC, _TQ), BF16)],
        compiler_params=_cparams(("parallel", "parallel", "arbitrary")),
        name="gqa_attention",
    )(qt, k_proc, k_proc, vt, vt)


_ML_CHUNKS = (SC + S) // ML_L
_ML_CTX_CHUNKS = SC // ML_L
_NEG = -1e30


def _log_sigmoid(x):
    return jnp.minimum(x, 0.0) - jnp.log1p(jnp.exp(-jnp.abs(x)))


def _exact_dot_f32(a01, x):
    hi = x.astype(BF16)
    r1 = x - hi.astype(F32)
    mid = r1.astype(BF16)
    lo = (r1 - mid.astype(F32)).astype(BF16)
    return (jnp.dot(a01, hi, preferred_element_type=F32)
            + jnp.dot(a01, mid, preferred_element_type=F32)
            + jnp.dot(a01, lo, preferred_element_type=F32))


def _gate_lane(kind, d, h):
    return kind * 2 * ML_H + d * ML_H + h


def _mlstm_kernel(qf_ref, kf_ref, vf_ref, gf_ref, qb_ref, kb_ref, vb_ref, gb_ref, bias_ref,
                  hf_ref, hb_ref, ct_scr, n_scr, m_scr):
    @pl.when(pl.program_id(1) == 0)
    def _():
        ct_scr[...] = jnp.zeros_like(ct_scr)
        n_scr[...] = jnp.zeros_like(n_scr)
        m_scr[...] = jnp.zeros_like(m_scr)

    L = ML_L
    row = lax.broadcasted_iota(jnp.int32, (L, L), 0)
    col = lax.broadcasted_iota(jnp.int32, (L, L), 1)
    q_scale = ML_DK ** -0.5

    for d, (q_ref, k_ref, v_ref, g_ref, h_ref) in enumerate(
            ((qf_ref, kf_ref, vf_ref, gf_ref, hf_ref), (qb_ref, kb_ref, vb_ref, gb_ref, hb_ref))):
        mask = (col <= row) if d == 0 else (col >= row)
        tri = jnp.where(mask, 1.0, 0.0).astype(BF16)
        gates = g_ref[...].astype(F32) + bias_ref[...]
        log_f = _log_sigmoid(gates)
        cum_lf = _exact_dot_f32(tri, log_f)
        gates_t = gates.T
        cum_lf_t = cum_lf.T
        tot_lf = jnp.sum(log_f, axis=0, keepdims=True)
        for h in range(ML_H):
            idx = d * ML_H + h
            li_lane = _gate_lane(0, d, h)
            lf_lane = _gate_lane(1, d, h)
            b_c = cum_lf[:, lf_lane:lf_lane + 1]
            b_r = cum_lf_t[lf_lane:lf_lane + 1, :]
            li_c = gates[:, li_lane:li_lane + 1]
            li_r = gates_t[li_lane:li_lane + 1, :]
            g_tot = tot_lf[:, lf_lane:lf_lane + 1]
            m_prev = m_scr[idx][0:1, 0:1]
            n_prev = n_scr[idx][0:1, :]
            ct_prev = ct_scr[idx]

            q = (q_ref[:, h * ML_DK:(h + 1) * ML_DK].astype(F32) * q_scale).astype(BF16)
            k = k_ref[:, h * ML_DK:(h + 1) * ML_DK]
            v = v_ref[:, h * ML_DV:(h + 1) * ML_DV]

            qk = lax.dot_general(q, k, (((1,), (1,)), ((), ())), preferred_element_type=F32)
            d_log = jnp.where(mask, b_c - b_r + li_r, _NEG)
            a_log = b_c + m_prev
            m_j = jnp.maximum(a_log, jnp.max(d_log, axis=-1, keepdims=True))
            w = jnp.exp(d_log - m_j) * qk
            inter = jnp.exp(a_log - m_j)
            num = (jnp.dot(w.astype(BF16), v, preferred_element_type=F32)
                   + inter * jnp.dot(q, ct_prev.astype(BF16), preferred_element_type=F32))
            nq = jnp.sum(q.astype(F32) * n_prev, axis=-1, keepdims=True)
            den = jnp.sum(w, axis=-1, keepdims=True) + inter * nq
            h_out = num / jnp.maximum(jnp.abs(den), jnp.exp(-m_j))
            h_ref[:, h * ML_DV:(h + 1) * ML_DV] = h_out.astype(h_ref.dtype)

            w_end = g_tot - b_c + li_c
            m_loc = jnp.max(w_end, axis=0, keepdims=True)
            e_c = jnp.exp(w_end - m_loc)
            m_new = jnp.maximum(g_tot + m_prev, m_loc)
            a_old = jnp.exp(g_tot + m_prev - m_new)
            a_loc = jnp.exp(m_loc - m_new)
            ev = (e_c * v.astype(F32)).astype(BF16)
            c_loc = lax.dot_general(k, ev, (((0,), (0,)), ((), ())), preferred_element_type=F32)
            n_loc = jnp.sum(e_c * k.astype(F32), axis=0, keepdims=True)
            ct_scr[idx] = a_old * ct_prev + a_loc * c_loc
            n_scr[idx] = jnp.broadcast_to(a_old * n_prev + a_loc * n_loc, (8, ML_DK))
            m_scr[idx] = jnp.broadcast_to(m_new, (8, LANES))


def _mlstm_call(p, gate_bias):
    lat_chunks = S // ML_L
    ctx_blk = N_LAT // ML_L

    def fwd_rb(b, t):
        return jnp.where(t < _ML_CTX_CHUNKS, ctx_blk + b * _ML_CTX_CHUNKS + t,
                         b * lat_chunks + t - _ML_CTX_CHUNKS)

    def rev_rb(b, t):
        return jnp.where(t < _ML_CTX_CHUNKS, ctx_blk + b * _ML_CTX_CHUNKS + (_ML_CTX_CHUNKS - 1 - t),
                         b * lat_chunks + (_ML_CHUNKS - 1 - t))

    def specs(rb):
        return [pl.BlockSpec((ML_L, ML_QKW), lambda b, t: (rb(b, t), _P_MLQ // ML_QKW)),
                pl.BlockSpec((ML_L, ML_QKW), lambda b, t: (rb(b, t), _P_MLK // ML_QKW)),
                pl.BlockSpec((ML_L, ML_VW), lambda b, t: (rb(b, t), _P_MLV // ML_VW)),
                pl.BlockSpec((ML_L, LANES), lambda b, t: (rb(b, t), _P_IF // LANES))]

    n_chain = 2 * ML_H
    return pl.pallas_call(
        _mlstm_kernel,
        out_shape=(jax.ShapeDtypeStruct((N_TOK, ML_VW), BF16),
                   jax.ShapeDtypeStruct((N_TOK, ML_VW), BF16)),
        grid=(B, _ML_CHUNKS),
        in_specs=specs(fwd_rb) + specs(rev_rb) + [pl.BlockSpec((1, LANES), lambda b, t: (0, 0))],
        out_specs=(pl.BlockSpec((ML_L, ML_VW), lambda b, t: (fwd_rb(b, t), 0)),
                   pl.BlockSpec((ML_L, ML_VW), lambda b, t: (rev_rb(b, t), 0))),
        scratch_shapes=[pltpu.VMEM((n_chain, ML_DK, ML_DV), F32),
                        pltpu.VMEM((n_chain, 8, ML_DK), F32),
                        pltpu.VMEM((n_chain, 8, LANES), F32)],
        compiler_params=_cparams(("parallel", "arbitrary")),
        name="mlstm",
    )(p, p, p, p, p, p, p, p, gate_bias)


def _pool_kernel(u_ref, w_ref, sc_ref, o_ref, mixed_scr, *, seq):
    gi = pl.program_id(1)
    t = lax.broadcasted_iota(jnp.int32, (seq, 1), 0)

    def prev(x, k):
        return jnp.where(t >= k, pltpu.roll(x, k, 0), 0.0)

    def nxt(x, k):
        return jnp.where(t < seq - k, pltpu.roll(x, seq - k, 0), 0.0)

    half_w = jnp.where(gi == 0, 1, jnp.where(gi == 1, 2, jnp.where(gi == 2, 4, 8)))
    count = (jnp.minimum(t + half_w, seq) - jnp.maximum(t - half_w, 0)).astype(F32)
    for c in range(POOL_G // LANES):
        cs = slice(c * LANES, (c + 1) * LANES)
        u = u_ref[:, cs].astype(F32)
        before = prev(u, 1)
        after = u
        win = before + after
        for step, k in enumerate((1, 2, 4)):
            before = before + prev(before, k)
            after = after + nxt(after, k)
            win = jnp.where(gi == step + 1, before + after, win)
        mixed_scr[:, cs] = (win / count - u).astype(BF16)
    o_ref[...] = (jnp.dot(mixed_scr[...], w_ref[...], preferred_element_type=F32)
                  * sc_ref[...]).astype(o_ref.dtype)


def _pool_call(p, pool_w, pool_scale, prev_out, layer, *, seq, first_blk):
    n_groups = len(POOL_WINDOWS)
    in_specs = [pl.BlockSpec((seq, POOL_G), lambda b, g: (first_blk + b, _P_POOL // POOL_G + g)),
                pl.BlockSpec((None, None, POOL_G, POOL_G), lambda b, g: (layer, g, 0, 0)),
                pl.BlockSpec((None, 1, POOL_G), lambda b, g: (layer, 0, g))]
    args = [p, pool_w, pool_scale]
    aliases = {}
    if prev_out is not None:
        in_specs.append(pl.BlockSpec(memory_space=pl.ANY))
        args.append(prev_out)
        aliases = {3: 0}

    def kern(u_ref, w_ref, sc_ref, *rest):
        o_ref, mixed_scr = rest[-2], rest[-1]
        _pool_kernel(u_ref, w_ref, sc_ref, o_ref, mixed_scr, seq=seq)

    return pl.pallas_call(
        kern,
        out_shape=jax.ShapeDtypeStruct((N_TOK, POOL_W), BF16),
        grid=(B, n_groups),
        in_specs=in_specs,
        out_specs=pl.BlockSpec((seq, POOL_G), lambda b, g: (first_blk + b, g)),
        scratch_shapes=[pltpu.VMEM((seq, POOL_G), BF16)],
        input_output_aliases=aliases,
        compiler_params=_cparams(("parallel", "parallel")),
        name="pool_branch_%d" % seq,
    )(*args)


def _merge_kernel(att_ref, hf_ref, hb_ref, og_ref, pool_ref, g0_ref, g1_ref, g2_ref,
                  wa_ref, wm_ref, wp_ref, mlg_ref, y_ref, ml_scr):
    @pl.when(pl.program_id(1) == 0)
    def _():
        for h in range(ML_H):
            cs = slice(h * ML_DV, (h + 1) * ML_DV)
            hs = hf_ref[:, cs].astype(F32) + hb_ref[:, cs].astype(F32)
            ms = jnp.mean(hs * hs, axis=-1, keepdims=True)
            normed = hs * lax.rsqrt(ms + EPS) * mlg_ref[:, cs]
            ml_scr[:, cs] = (normed * og_ref[:, cs].astype(F32)).astype(BF16)

    def branch(g_ref, a, w_ref):
        return g_ref[...].astype(F32) * jnp.dot(a, w_ref[...], preferred_element_type=F32)

    y = (branch(g0_ref, att_ref[...], wa_ref) + branch(g1_ref, ml_scr[...], wm_ref)
         + branch(g2_ref, pool_ref[...], wp_ref))
    y_ref[...] = y.astype(y_ref.dtype)


def _merge_call(att, hf, hb, p, pool, w_up_att, w_up_ml, w_up_pool, ml_head_g, layer, *, rows):
    tm, tn = 512, 1024
    nj = D // tn

    def row_blk(width):
        return pl.BlockSpec((tm, width), lambda i, j: (i, 0))

    def gate_blk(branch):
        return pl.BlockSpec((tm, tn), lambda i, j: (i, (_P_GATE + branch * D) // tn + j))

    def w_blk(k):
        return pl.BlockSpec((None, k, tn), lambda i, j: (layer, 0, j))

    return pl.pallas_call(
        _merge_kernel,
        out_shape=jax.ShapeDtypeStruct((rows, D), BF16),
        grid=(rows // tm, nj),
        in_specs=[row_blk(ATT_QW), row_blk(ML_VW), row_blk(ML_VW),
                  pl.BlockSpec((tm, ML_VW), lambda i, j: (i, _P_O // ML_VW)),
                  row_blk(POOL_W), gate_blk(0), gate_blk(1), gate_blk(2),
                  w_blk(ATT_QW), w_blk(ML_VW), w_blk(POOL_W),
                  pl.BlockSpec((None, 1, ML_VW), lambda i, j: (layer, 0, 0))],
        out_specs=pl.BlockSpec((tm, tn), lambda i, j: (i, j)),
        scratch_shapes=[pltpu.VMEM((tm, ML_VW), BF16)],
        compiler_params=_cparams(("parallel", "arbitrary")),
        name="branch_merge",
    )(att, hf, hb, p, pool, p, p, p, w_up_att, w_up_ml, w_up_pool, ml_head_g)


def _regroup_w_in(w):
    widths = (ATT_KVW, ATT_KVW, ML_QKW, ML_VW, 4 * ML_H, ATT_QW, ML_QKW, ML_VW, POOL_W, 3 * D)
    parts, o = [], 0
    for width in widths:
        parts.append(w[..., o:o + width].astype(BF16))
        o += width
    k_w, v_w, mlk_w, mlv_w, if_w, q_w, mlq_w, og_w, pool_w, gate_w = parts
    pad = jnp.zeros(w.shape[:2] + (_P_W - _P_IF - 4 * ML_H,), BF16)
    return jnp.concatenate([gate_w, og_w, pool_w, mlv_w, q_w, mlk_w, mlq_w, k_w, v_w, if_w, pad], axis=-1)


def _rope_tables():
    t = jnp.arange(S)
    row = (t // GRID_W).astype(F32)
    colp = (t % GRID_W).astype(F32)
    n_freq = HD // 4
    inv = ROPE_THETA ** (-jnp.arange(n_freq, dtype=F32) / n_freq)
    ar, ac = row[:, None] * inv, colp[:, None] * inv
    cos_t = jnp.concatenate([jnp.cos(ar), jnp.cos(ar), jnp.cos(ac), jnp.cos(ac)], axis=1)
    sin_t = jnp.concatenate([-jnp.sin(ar), jnp.sin(ar), -jnp.sin(ac), jnp.sin(ac)], axis=1)
    return cos_t, sin_t


def kernel(x, c, ctx, c_ctx, w_mod, b_mod, norm_g, w_in, ml_gate_b, qk_norm_g, ml_head_g,
           pool_w, pool_scale, w_up_att, w_up_ml, w_up_pool, w_out, w_ffn_in, w_ffn_out):
    cos_t, sin_t = _rope_tables()
    xa = jnp.concatenate([x.reshape(N_LAT, D), ctx.reshape(N_CTX, D)], axis=0)
    c8 = jnp.concatenate([c, c_ctx[None], jnp.zeros((8 - B - 1, D), F32)], axis=0)

    w_in_r = _regroup_w_in(w_in)
    w_out_b = w_out.astype(BF16)
    w_ffn_out_b = w_ffn_out.astype(BF16)
    w_up_att_b, w_up_ml_b, w_up_pool_b = (w.astype(BF16) for w in (w_up_att, w_up_ml, w_up_pool))
    pool_w_b = pool_w.astype(BF16)
    pool_scale_r = pool_scale.reshape(DEPTH, 1, POOL_W)
    ml_head_g_r = ml_head_g.reshape(DEPTH, 1, ML_VW)
    gate_bias = jnp.pad(jnp.transpose(ml_gate_b, (0, 2, 1, 3)).reshape(DEPTH, 1, 4 * ML_H),
                        ((0, 0), (0, 0), (0, LANES - 4 * ML_H)))

    for l in range(DEPTH):
        last = l == DEPTH - 1
        rows = N_LAT if last else N_TOK

        mod = _mod_call(c8, w_mod, b_mod, l).reshape(8, 6, 1, D)
        sh_m, sc_m, gt_m, sh_f, sc_f, gt_f = (mod[:, i] for i in range(6))

        p = _in_proj_call(xa, norm_g[l, 0][None], sh_m, sc_m, w_in_r, l)

        qt, k_proc, vt = _qkprep_call(p, cos_t, sin_t, qk_norm_g[l, 0][None], qk_norm_g[l, 1][None])
        att = _attn_call(qt, k_proc, vt, ctx_step=not last)

        hf, hb = _mlstm_call(p, gate_bias[l])

        pool = _pool_call(p, pool_w_b, pool_scale_r, None, l, seq=S, first_blk=0)
        if not last:
            pool = _pool_call(p, pool_w_b, pool_scale_r, pool, l, seq=SC, first_blk=N_LAT // SC)

        y = _merge_call(att, hf, hb, p, pool, w_up_att_b, w_up_ml_b, w_up_pool_b, ml_head_g_r, l, rows=rows)
        xa = _resid_call(y, w_out_b, xa, norm_g[l, 1][None], gt_m, l, rows=rows, tm=512, name="out_proj")

        f = _ffn_in_call(xa, norm_g[l, 2][None], sh_f, sc_f, w_ffn_in, l, rows=rows)
        xa = _resid_call(f, w_ffn_out_b, xa, norm_g[l, 3][None], gt_f, l, rows=rows, tm=256, name="ffn_out")

    return xa.reshape(B, S, D)
```

```python
import functools
import math

import jax
import jax.numpy as jnp
from jax import lax
from jax.experimental import pallas as pl
from jax.experimental.pallas import tpu as pltpu

F32 = jnp.float32
BF16 = jnp.bfloat16

D = 2048
B = 4
S = 4096
SC = 256
DEPTH = 2
GRID_W = 64
ATT_H = 8
ATT_KVH = 2
ATT_G = ATT_H // ATT_KVH
HD = 128
ROPE_THETA = 10000.0
ML_H = 4
ML_DK = 128
ML_DV = 256
ML_L = 128
POOL_WINDOWS = (2, 4, 8, 16)
POOL_G = 256
D_FF = 5632
EPS = 1e-6

N_LAT = B * S
N_CTX = B * SC
N_TOK = N_LAT + N_CTX

ATT_QW = ATT_H * HD
ATT_KVW = ATT_KVH * HD
ML_QKW = ML_H * ML_DK
ML_VW = ML_H * ML_DV
POOL_W = len(POOL_WINDOWS) * POOL_G

_P_GATE = 0
_P_O = 3 * D
_P_POOL = _P_O + ML_VW
_P_MLV = _P_POOL + POOL_W
_P_Q = _P_MLV + ML_VW
_P_MLK = _P_Q + ATT_QW
_P_MLQ = _P_MLK + ML_QKW
_P_K = _P_MLQ + ML_QKW
_P_V = _P_K + ATT_KVW
_P_IF = _P_V + ATT_KVW
_P_W = 12288
_P_TN = 1024
_P_SIGMOID_TILES = (_P_O + ML_VW) // _P_TN
LANES = 128

VMEM_LIMIT = 56 * 1024 * 1024


def _cparams(sem, vmem=VMEM_LIMIT):
    return pltpu.CompilerParams(dimension_semantics=sem, vmem_limit_bytes=vmem)


def _sigmoid(x):
    return 0.5 * jnp.tanh(0.5 * x) + 0.5


def _mod_kernel(c_ref, w_ref, b_ref, o_ref):
    c = c_ref[...]
    sc = (c * _sigmoid(c)).astype(BF16)
    o_ref[...] = jnp.dot(sc, w_ref[...].astype(BF16), preferred_element_type=F32) + b_ref[...]


def _mod_call(c8, w_mod, b_mod, layer):
    tn = 1024
    n = w_mod.shape[2]
    return pl.pallas_call(
        _mod_kernel,
        out_shape=jax.ShapeDtypeStruct((8, n), F32),
        grid=(n // tn,),
        in_specs=[pl.BlockSpec((8, D), lambda j: (0, 0)),
                  pl.BlockSpec((None, D, tn), lambda j: (layer, 0, j)),
                  pl.BlockSpec((None, 1, tn), lambda j: (layer, 0, j))],
        out_specs=pl.BlockSpec((8, tn), lambda j: (0, j)),
        compiler_params=_cparams(("parallel",)),
        name="mod_proj",
    )(c8, w_mod, b_mod.reshape(DEPTH, 1, n))


_ROW_CHUNK = 64


def _norm_modulate_rows(x_ref, g_ref, sh_ref, sc_ref, h_ref, tm):
    g = g_ref[...]
    one_plus = 1.0 + sc_ref[...]
    sh = sh_ref[...]

    def body(r, carry):
        rows = pl.ds(pl.multiple_of(r * _ROW_CHUNK, _ROW_CHUNK), _ROW_CHUNK)
        xf = x_ref[rows, :]
        ms = jnp.mean(xf * xf, axis=-1, keepdims=True)
        y = xf * lax.rsqrt(ms + EPS) * g
        h_ref[rows, :] = (y * one_plus + sh).astype(BF16)
        return carry

    lax.fori_loop(0, tm // _ROW_CHUNK, body, 0)


def _in_proj_kernel(x_ref, g_ref, sh_ref, sc_ref, w_ref, o_ref, h_scr, *, tm):
    j = pl.program_id(1)

    @pl.when(j == 0)
    def _():
        _norm_modulate_rows(x_ref, g_ref, sh_ref, sc_ref, h_scr, tm)

    acc = jnp.dot(h_scr[...], w_ref[...], preferred_element_type=F32)
    o_ref[...] = jnp.where(j < _P_SIGMOID_TILES, _sigmoid(acc), acc).astype(o_ref.dtype)


def _ffn_in_kernel(x_ref, g_ref, sh_ref, sc_ref, wg_ref, wu_ref, o_ref, h_scr, *, tm):
    @pl.when(pl.program_id(1) == 0)
    def _():
        _norm_modulate_rows(x_ref, g_ref, sh_ref, sc_ref, h_scr, tm)

    h = h_scr[...]
    a = jnp.dot(h, wg_ref[...].astype(BF16), preferred_element_type=F32)
    u = jnp.dot(h, wu_ref[...].astype(BF16), preferred_element_type=F32)
    o_ref[...] = (a * _sigmoid(a) * u).astype(o_ref.dtype)


def _row_specs(tm):
    return [pl.BlockSpec((tm, D), lambda i, j: (i, 0)),
            pl.BlockSpec((1, D), lambda i, j: (0, 0)),
            pl.BlockSpec((None, 1, D), lambda i, j: ((i * tm) // S, 0, 0)),
            pl.BlockSpec((None, 1, D), lambda i, j: ((i * tm) // S, 0, 0))]


def _in_proj_call(x, g, shift, scale, w, layer):
    tm, tn = 1024, _P_TN
    return pl.pallas_call(
        functools.partial(_in_proj_kernel, tm=tm),
        out_shape=jax.ShapeDtypeStruct((N_TOK, _P_W), BF16),
        grid=(N_TOK // tm, _P_W // tn),
        in_specs=_row_specs(tm) + [pl.BlockSpec((None, D, tn), lambda i, j: (layer, 0, j))],
        out_specs=pl.BlockSpec((tm, tn), lambda i, j: (i, j)),
        scratch_shapes=[pltpu.VMEM((tm, D), BF16)],
        compiler_params=_cparams(("parallel", "arbitrary")),
        name="in_proj",
    )(x, g, shift, scale, w)


def _ffn_in_call(x, g, shift, scale, w, layer, *, rows):
    tm, tn = 1024, 512
    nj = D_FF // tn
    return pl.pallas_call(
        functools.partial(_ffn_in_kernel, tm=tm),
        out_shape=jax.ShapeDtypeStruct((rows, D_FF), BF16),
        grid=(rows // tm, nj),
        in_specs=_row_specs(tm) + [pl.BlockSpec((None, D, tn), lambda i, j: (layer, 0, j)),
                                   pl.BlockSpec((None, D, tn), lambda i, j: (layer, 0, nj + j))],
        out_specs=pl.BlockSpec((tm, tn), lambda i, j: (i, j)),
        scratch_shapes=[pltpu.VMEM((tm, D), BF16)],
        compiler_params=_cparams(("parallel", "arbitrary")),
        name="ffn_in",
    )(x, g, shift, scale, w, w)


def _resid_kernel(a_ref, w_ref, x_ref, g_ref, gate_ref, o_ref):
    z = jnp.dot(a_ref[...], w_ref[...], preferred_element_type=F32)
    ms = jnp.mean(z * z, axis=-1, keepdims=True)
    o_ref[...] = x_ref[...] + gate_ref[...] * (z * lax.rsqrt(ms + EPS) * g_ref[...])


def _resid_call(a, w, x, g, gate, layer, *, rows, tm, name):
    k = w.shape[1]
    return pl.pallas_call(
        _resid_kernel,
        out_shape=jax.ShapeDtypeStruct((rows, D), F32),
        grid=(rows // tm,),
        in_specs=[pl.BlockSpec((tm, k), lambda i: (i, 0)),
                  pl.BlockSpec((None, k, D), lambda i: (layer, 0, 0), pipeline_mode=pl.Buffered(1)),
                  pl.BlockSpec((tm, D), lambda i: (i, 0)),
                  pl.BlockSpec((1, D), lambda i: (0, 0)),
                  pl.BlockSpec((None, 1, D), lambda i: ((i * tm) // S, 0, 0))],
        out_specs=pl.BlockSpec((tm, D), lambda i: (i, 0)),
        compiler_params=_cparams(("parallel",)),
        name=name,
    )(a, w, x, g, gate)


def _head_norm_rope(xh, g, cos, sin, first_half):
    xf = xh.astype(F32)
    ms = jnp.mean(xf * xf, axis=-1, keepdims=True)
    y = xf * lax.rsqrt(ms + EPS) * g
    swapped = jnp.where(first_half, pltpu.roll(y, HD - HD // 4, 1), pltpu.roll(y, HD // 4, 1))
    return y * cos + swapped * sin


def _qkprep_kernel(q_ref, k_ref, v_ref, cos_ref, sin_ref, gq_ref, gk_ref, qt_ref, ko_ref, vt_ref,
                   *, lat_tiles):
    use_rope = pl.program_id(0) < lat_tiles
    cos = jnp.where(use_rope, cos_ref[...], 1.0)
    sin = jnp.where(use_rope, sin_ref[...], 0.0)
    lane = lax.broadcasted_iota(jnp.int32, (1, HD), 1)
    first_half = (lane % (HD // 2)) < (HD // 4)
    q_scale = HD ** -0.5 * math.log2(math.e)
    for h in range(ATT_H):
        cs = slice(h * HD, (h + 1) * HD)
        y = _head_norm_rope(q_ref[:, cs], gq_ref[...], cos, sin, first_half)
        qt_ref[cs, :] = (y * q_scale).T.astype(BF16)
    for h in range(ATT_KVH):
        cs = slice(h * HD, (h + 1) * HD)
        y = _head_norm_rope(k_ref[:, cs], gk_ref[...], cos, sin, first_half)
        ko_ref[:, cs] = y.astype(BF16)
        vt_ref[cs, :] = v_ref[:, cs].astype(F32).T.astype(BF16)


def _qkprep_call(p, cos_t, sin_t, gq, gk):
    tm = 512
    lat_tiles = N_LAT // tm
    pos_tiles = S // tm
    return pl.pallas_call(
        functools.partial(_qkprep_kernel, lat_tiles=lat_tiles),
        out_shape=(jax.ShapeDtypeStruct((ATT_QW, N_TOK), BF16),
                   jax.ShapeDtypeStruct((N_TOK, ATT_KVW), BF16),
                   jax.ShapeDtypeStruct((ATT_KVW, N_TOK), BF16)),
        grid=(N_TOK // tm,),
        in_specs=[pl.BlockSpec((tm, ATT_QW), lambda i: (i, _P_Q // ATT_QW)),
                  pl.BlockSpec((tm, ATT_KVW), lambda i: (i, _P_K // ATT_KVW)),
                  pl.BlockSpec((tm, ATT_KVW), lambda i: (i, _P_V // ATT_KVW)),
                  pl.BlockSpec((tm, HD), lambda i: (i % pos_tiles, 0)),
                  pl.BlockSpec((tm, HD), lambda i: (i % pos_tiles, 0)),
                  pl.BlockSpec((1, HD), lambda i: (0, 0)),
                  pl.BlockSpec((1, HD), lambda i: (0, 0))],
        out_specs=(pl.BlockSpec((ATT_QW, tm), lambda i: (0, i)),
                   pl.BlockSpec((tm, ATT_KVW), lambda i: (i, 0)),
                   pl.BlockSpec((ATT_KVW, tm), lambda i: (0, i))),
        compiler_params=_cparams(("parallel",)),
        name="qk_prep",
    )(p, p, p, cos_t, sin_t, gq, gk)


_TQ = 256
_KC = 512


def _attention_heads(qt_ref, key_refs, vt_refs, o_ref, s_scr, p_scr):
    chunks = []
    spans = []
    base = 0
    for k_ref in key_refs:
        n_keys = k_ref.shape[0]
        kc = min(_KC, n_keys)
        chunks += [(k_ref, c * kc, kc, base + c * kc) for c in range(n_keys // kc)]
        spans.append((base, n_keys))
        base += n_keys

    def scores(h):
        slot = h % 2
        qt = qt_ref[h * HD:(h + 1) * HD, :]
        mx = None
        for k_ref, r0, nr, s0 in chunks:
            sc = jnp.dot(k_ref[r0:r0 + nr, :], qt, preferred_element_type=F32)
            s_scr[slot, s0:s0 + nr, :] = sc
            part = jnp.max(sc.reshape(nr // 8, 8, _TQ), axis=0)
            mx = part if mx is None else jnp.maximum(mx, part)
        return jnp.max(mx, axis=0, keepdims=True)

    def probs(h, m):
        slot = h % 2
        tot = None
        for _, _, nr, s0 in chunks:
            pr = jnp.exp2(s_scr[slot, s0:s0 + nr, :] - m)
            p_scr[slot, s0:s0 + nr, :] = pr.astype(BF16)
            part = jnp.sum(pr.reshape(nr // 8, 8, _TQ), axis=0)
            tot = part if tot is None else tot + part
        return jnp.sum(tot, axis=0, keepdims=True)

    def output(h, denom):
        slot = h % 2
        ot = None
        for vt_ref, (s0, nr) in zip(vt_refs, spans):
            part = jnp.dot(vt_ref[...], p_scr[slot, s0:s0 + nr, :], preferred_element_type=F32)
            ot = part if ot is None else ot + part
        o_ref[:, h * HD:(h + 1) * HD] = (ot * (1.0 / denom)).T.astype(o_ref.dtype)

    m = scores(0)
    for h in range(ATT_G):
        m_next = scores(h + 1) if h + 1 < ATT_G else None
        output(h, probs(h, m))
        m = m_next


def _attn_latent_kernel(qt_ref, kl_ref, kc_ref, vtl_ref, vtc_ref, o_ref, s_scr, p_scr):
    _attention_heads(qt_ref, (kl_ref, kc_ref), (vtl_ref, vtc_ref), o_ref, s_scr, p_scr)


def _attn_ctx_kernel(qt_ref, kc_ref, vtc_ref, prev_ref, o_ref, s_scr, p_scr):
    del prev_ref
    _attention_heads(qt_ref, (kc_ref,), (vtc_ref,), o_ref, s_scr, p_scr)


def _attn_call(qt, k_proc, vt, *, with_ctx_queries):
    lat_tiles = S // _TQ
    ctx_blk = N_LAT // SC
    gw = ATT_G * HD

    att = pl.pallas_call(
        _attn_latent_kernel,
        out_shape=jax.ShapeDtypeStruct((N_TOK, ATT_QW), BF16),
        grid=(B, ATT_KVH, lat_tiles),
        in_specs=[pl.BlockSpec((gw, _TQ), lambda b, kv, t: (kv, b * lat_tiles + t)),
                  pl.BlockSpec((S, HD), lambda b, kv, t: (b, kv)),
                  pl.BlockSpec((SC, HD), lambda b, kv, t: (ctx_blk + b, kv)),
                  pl.BlockSpec((HD, S), lambda b, kv, t: (kv, b)),
                  pl.BlockSpec((HD, SC), lambda b, kv, t: (kv, ctx_blk + b))],
        out_specs=pl.BlockSpec((_TQ, gw), lambda b, kv, t: (b * lat_tiles + t, kv)),
        scratch_shapes=[pltpu.VMEM((2, S + SC, _TQ), F32),
                        pltpu.VMEM((2, S + SC, _TQ), BF16)],
        compiler_params=_cparams(("parallel", "parallel", "arbitrary")),
        name="gqa_attention",
    )(qt, k_proc, k_proc, vt, vt)
    if not with_ctx_queries:
        return att
    return pl.pallas_call(
        _attn_ctx_kernel,
        out_shape=jax.ShapeDtypeStruct((N_TOK, ATT_QW), BF16),
        grid=(B, ATT_KVH),
        in_specs=[pl.BlockSpec((gw, SC), lambda b, kv: (kv, ctx_blk + b)),
                  pl.BlockSpec((SC, HD), lambda b, kv: (ctx_blk + b, kv)),
                  pl.BlockSpec((HD, SC), lambda b, kv: (kv, ctx_blk + b)),
                  pl.BlockSpec(memory_space=pl.ANY)],
        out_specs=pl.BlockSpec((SC, gw), lambda b, kv: (ctx_blk + b, kv)),
        scratch_shapes=[pltpu.VMEM((2, SC, _TQ), F32),
                        pltpu.VMEM((2, SC, _TQ), BF16)],
        input_output_aliases={3: 0},
        compiler_params=_cparams(("parallel", "parallel")),
        name="gqa_attention_ctx",
    )(qt, k_proc, vt, att)


_ML_CHUNKS = (SC + S) // ML_L
_ML_CTX_CHUNKS = SC // ML_L
_NEG = -1e30


def _log_sigmoid(x):
    return jnp.minimum(x, 0.0) - jnp.log1p(jnp.exp(-jnp.abs(x)))


def _exact_dot_f32(a01, x):
    hi = x.astype(BF16)
    r1 = x - hi.astype(F32)
    mid = r1.astype(BF16)
    lo = (r1 - mid.astype(F32)).astype(BF16)
    return (jnp.dot(a01, hi, preferred_element_type=F32)
            + jnp.dot(a01, mid, preferred_element_type=F32)
            + jnp.dot(a01, lo, preferred_element_type=F32))


def _gate_lane(kind, d, h):
    return kind * 2 * ML_H + d * ML_H + h


def _mlstm_kernel(qf_ref, kf_ref, vf_ref, gf_ref, qb_ref, kb_ref, vb_ref, gb_ref, bias_ref,
                  hf_ref, hb_ref, ct_scr, n_scr, m_scr):
    @pl.when(pl.program_id(1) == 0)
    def _():
        ct_scr[...] = jnp.zeros_like(ct_scr)
        n_scr[...] = jnp.zeros_like(n_scr)
        m_scr[...] = jnp.zeros_like(m_scr)

    L = ML_L
    row = lax.broadcasted_iota(jnp.int32, (L, L), 0)
    col = lax.broadcasted_iota(jnp.int32, (L, L), 1)
    q_scale = ML_DK ** -0.5

    for d, (q_ref, k_ref, v_ref, g_ref, h_ref) in enumerate(
            ((qf_ref, kf_ref, vf_ref, gf_ref, hf_ref), (qb_ref, kb_ref, vb_ref, gb_ref, hb_ref))):
        mask = (col <= row) if d == 0 else (col >= row)
        tri = jnp.where(mask, 1.0, 0.0).astype(BF16)
        gates = g_ref[...].astype(F32) + bias_ref[...]
        log_f = _log_sigmoid(gates)
        cum_lf = _exact_dot_f32(tri, log_f)
        gates_t = gates.T
        cum_lf_t = cum_lf.T
        tot_lf = jnp.sum(log_f, axis=0, keepdims=True)
        for h in range(ML_H):
            idx = d * ML_H + h
            li_lane = _gate_lane(0, d, h)
            lf_lane = _gate_lane(1, d, h)
            b_c = cum_lf[:, lf_lane:lf_lane + 1]
            b_r = cum_lf_t[lf_lane:lf_lane + 1, :]
            li_c = gates[:, li_lane:li_lane + 1]
            li_r = gates_t[li_lane:li_lane + 1, :]
            g_tot = tot_lf[:, lf_lane:lf_lane + 1]
            m_prev = m_scr[idx][0:1, 0:1]
            n_prev = n_scr[idx][0:1, :]
            ct_prev = ct_scr[idx]

            q = (q_ref[:, h * ML_DK:(h + 1) * ML_DK].astype(F32) * q_scale).astype(BF16)
            k = k_ref[:, h * ML_DK:(h + 1) * ML_DK]
            v = v_ref[:, h * ML_DV:(h + 1) * ML_DV]

            qk = lax.dot_general(q, k, (((1,), (1,)), ((), ())), preferred_element_type=F32)
            d_log = jnp.where(mask, b_c - b_r + li_r, _NEG)
            a_log = b_c + m_prev
            m_j = jnp.maximum(a_log, jnp.max(d_log, axis=-1, keepdims=True))
            w = jnp.exp(d_log - m_j) * qk
            inter = jnp.exp(a_log - m_j)
            num = (jnp.dot(w.astype(BF16), v, preferred_element_type=F32)
                   + inter * jnp.dot(q, ct_prev.astype(BF16), preferred_element_type=F32))
            nq = jnp.sum(q.astype(F32) * n_prev, axis=-1, keepdims=True)
            den = jnp.sum(w, axis=-1, keepdims=True) + inter * nq
            h_out = num / jnp.maximum(jnp.abs(den), jnp.exp(-m_j))
            h_ref[:, h * ML_DV:(h + 1) * ML_DV] = h_out.astype(h_ref.dtype)

            w_end = g_tot - b_c + li_c
            m_loc = jnp.max(w_end, axis=0, keepdims=True)
            e_c = jnp.exp(w_end - m_loc)
            m_new = jnp.maximum(g_tot + m_prev, m_loc)
            a_old = jnp.exp(g_tot + m_prev - m_new)
            a_loc = jnp.exp(m_loc - m_new)
            ev = (e_c * v.astype(F32)).astype(BF16)
            c_loc = lax.dot_general(k, ev, (((0,), (0,)), ((), ())), preferred_element_type=F32)
            n_loc = jnp.sum(e_c * k.astype(F32), axis=0, keepdims=True)
            ct_scr[idx] = a_old * ct_prev + a_loc * c_loc
            n_scr[idx] = jnp.broadcast_to(a_old * n_prev + a_loc * n_loc, (8, ML_DK))
            m_scr[idx] = jnp.broadcast_to(m_new, (8, LANES))


def _mlstm_call(p, gate_bias):
    lat_chunks = S // ML_L
    ctx_blk = N_LAT // ML_L

    def fwd_rb(b, t):
        return jnp.where(t < _ML_CTX_CHUNKS, ctx_blk + b * _ML_CTX_CHUNKS + t,
                         b * lat_chunks + t - _ML_CTX_CHUNKS)

    def rev_rb(b, t):
        return jnp.where(t < _ML_CTX_CHUNKS, ctx_blk + b * _ML_CTX_CHUNKS + (_ML_CTX_CHUNKS - 1 - t),
                         b * lat_chunks + (_ML_CHUNKS - 1 - t))

    def specs(rb):
        return [pl.BlockSpec((ML_L, ML_QKW), lambda b, t: (rb(b, t), _P_MLQ // ML_QKW)),
                pl.BlockSpec((ML_L, ML_QKW), lambda b, t: (rb(b, t), _P_MLK // ML_QKW)),
                pl.BlockSpec((ML_L, ML_VW), lambda b, t: (rb(b, t), _P_MLV // ML_VW)),
                pl.BlockSpec((ML_L, LANES), lambda b, t: (rb(b, t), _P_IF // LANES))]

    n_chain = 2 * ML_H
    return pl.pallas_call(
        _mlstm_kernel,
        out_shape=(jax.ShapeDtypeStruct((N_TOK, ML_VW), BF16),
                   jax.ShapeDtypeStruct((N_TOK, ML_VW), BF16)),
        grid=(B, _ML_CHUNKS),
        in_specs=specs(fwd_rb) + specs(rev_rb) + [pl.BlockSpec((1, LANES), lambda b, t: (0, 0))],
        out_specs=(pl.BlockSpec((ML_L, ML_VW), lambda b, t: (fwd_rb(b, t), 0)),
                   pl.BlockSpec((ML_L, ML_VW), lambda b, t: (rev_rb(b, t), 0))),
        scratch_shapes=[pltpu.VMEM((n_chain, ML_DK, ML_DV), F32),
                        pltpu.VMEM((n_chain, 8, ML_DK), F32),
                        pltpu.VMEM((n_chain, 8, LANES), F32)],
        compiler_params=_cparams(("parallel", "arbitrary")),
        name="mlstm",
    )(p, p, p, p, p, p, p, p, gate_bias)


def _pool_kernel(u_ref, w_ref, sc_ref, o_ref, mixed_scr, *, seq):
    gi = pl.program_id(1)
    t = lax.broadcasted_iota(jnp.int32, (seq, 1), 0)

    def prev(x, k):
        return jnp.where(t >= k, pltpu.roll(x, k, 0), 0.0)

    def nxt(x, k):
        return jnp.where(t < seq - k, pltpu.roll(x, seq - k, 0), 0.0)

    half_w = jnp.where(gi == 0, 1, jnp.where(gi == 1, 2, jnp.where(gi == 2, 4, 8)))
    count = (jnp.minimum(t + half_w, seq) - jnp.maximum(t - half_w, 0)).astype(F32)
    for c in range(POOL_G // LANES):
        cs = slice(c * LANES, (c + 1) * LANES)
        u = u_ref[:, cs].astype(F32)
        before = prev(u, 1)
        after = u
        win = before + after
        for step, k in enumerate((1, 2, 4)):
            before = before + prev(before, k)
            after = after + nxt(after, k)
            win = jnp.where(gi == step + 1, before + after, win)
        mixed_scr[:, cs] = (win / count - u).astype(BF16)
    o_ref[...] = (jnp.dot(mixed_scr[...], w_ref[...], preferred_element_type=F32)
                  * sc_ref[...]).astype(o_ref.dtype)


def _pool_call(p, pool_w, pool_scale, prev_out, layer, *, seq, first_blk):
    n_groups = len(POOL_WINDOWS)
    in_specs = [pl.BlockSpec((seq, POOL_G), lambda b, g: (first_blk + b, _P_POOL // POOL_G + g)),
                pl.BlockSpec((None, None, POOL_G, POOL_G), lambda b, g: (layer, g, 0, 0)),
                pl.BlockSpec((None, 1, POOL_G), lambda b, g: (layer, 0, g))]
    args = [p, pool_w, pool_scale]
    aliases = {}
    if prev_out is not None:
        in_specs.append(pl.BlockSpec(memory_space=pl.ANY))
        args.append(prev_out)
        aliases = {3: 0}

    def kern(u_ref, w_ref, sc_ref, *rest):
        o_ref, mixed_scr = rest[-2], rest[-1]
        _pool_kernel(u_ref, w_ref, sc_ref, o_ref, mixed_scr, seq=seq)

    return pl.pallas_call(
        kern,
        out_shape=jax.ShapeDtypeStruct((N_TOK, POOL_W), BF16),
        grid=(B, n_groups),
        in_specs=in_specs,
        out_specs=pl.BlockSpec((seq, POOL_G), lambda b, g: (first_blk + b, g)),
        scratch_shapes=[pltpu.VMEM((seq, POOL_G), BF16)],
        input_output_aliases=aliases,
        compiler_params=_cparams(("parallel", "parallel")),
        name="pool_branch_%d" % seq,
    )(*args)


def _merge_kernel(att_ref, hf_ref, hb_ref, og_ref, pool_ref, g0_ref, g1_ref, g2_ref,
                  wa_ref, wm_ref, wp_ref, wo_ref, mlg_ref, x_ref, ng_ref, gate_ref, o_ref, ml_scr):
    for h in range(ML_H):
        cs = slice(h * ML_DV, (h + 1) * ML_DV)
        hs = hf_ref[:, cs].astype(F32) + hb_ref[:, cs].astype(F32)
        ms = jnp.mean(hs * hs, axis=-1, keepdims=True)
        normed = hs * lax.rsqrt(ms + EPS) * mlg_ref[:, cs]
        ml_scr[:, cs] = (normed * og_ref[:, cs].astype(F32)).astype(BF16)

    def branch(g_ref, a, w_ref):
        return g_ref[...].astype(F32) * jnp.dot(a, w_ref[...], preferred_element_type=F32)

    y = (branch(g0_ref, att_ref[...], wa_ref) + branch(g1_ref, ml_scr[...], wm_ref)
         + branch(g2_ref, pool_ref[...], wp_ref))
    z = jnp.dot(y.astype(BF16), wo_ref[...], preferred_element_type=F32)
    ms = jnp.mean(z * z, axis=-1, keepdims=True)
    o_ref[...] = x_ref[...] + gate_ref[...] * (z * lax.rsqrt(ms + EPS) * ng_ref[...])


def _merge_call(att, hf, hb, p, pool, w_up_att, w_up_ml, w_up_pool, w_out, ml_head_g, x, ng, gate,
                layer, *, rows):
    tm = 256

    def row_blk(width, col=0):
        return pl.BlockSpec((tm, width), lambda i: (i, col))

    def w_blk(k):
        return pl.BlockSpec((None, k, D), lambda i: (layer, 0, 0), pipeline_mode=pl.Buffered(1))

    return pl.pallas_call(
        _merge_kernel,
        out_shape=jax.ShapeDtypeStruct((rows, D), F32),
        grid=(rows // tm,),
        in_specs=[row_blk(ATT_QW), row_blk(ML_VW), row_blk(ML_VW), row_blk(ML_VW, _P_O // ML_VW),
                  row_blk(POOL_W),
                  row_blk(D, _P_GATE // D), row_blk(D, _P_GATE // D + 1), row_blk(D, _P_GATE // D + 2),
                  w_blk(ATT_QW), w_blk(ML_VW), w_blk(POOL_W), w_blk(D),
                  pl.BlockSpec((None, 1, ML_VW), lambda i: (layer, 0, 0)),
                  row_blk(D),
                  pl.BlockSpec((1, D), lambda i: (0, 0)),
                  pl.BlockSpec((None, 1, D), lambda i: ((i * tm) // S, 0, 0))],
        out_specs=pl.BlockSpec((tm, D), lambda i: (i, 0)),
        scratch_shapes=[pltpu.VMEM((tm, ML_VW), BF16)],
        compiler_params=_cparams(("parallel",)),
        name="merge_out",
    )(att, hf, hb, p, pool, p, p, p, w_up_att, w_up_ml, w_up_pool, w_out, ml_head_g, x, ng, gate)


def _regroup_w_in(w):
    widths = (ATT_KVW, ATT_KVW, ML_QKW, ML_VW, 4 * ML_H, ATT_QW, ML_QKW, ML_VW, POOL_W, 3 * D)
    parts, o = [], 0
    for width in widths:
        parts.append(w[..., o:o + width])
        o += width
    k_w, v_w, mlk_w, mlv_w, if_w, q_w, mlq_w, og_w, pool_w, gate_w = parts
    pad = jnp.zeros(w.shape[:2] + (_P_W - _P_IF - 4 * ML_H,), w.dtype)
    return jnp.concatenate([gate_w, og_w, pool_w, mlv_w, q_w, mlk_w, mlq_w, k_w, v_w, if_w, pad],
                           axis=-1).astype(BF16)


def _rope_tables():
    t = jnp.arange(S)
    row = (t // GRID_W).astype(F32)
    colp = (t % GRID_W).astype(F32)
    n_freq = HD // 4
    inv = ROPE_THETA ** (-jnp.arange(n_freq, dtype=F32) / n_freq)
    ar, ac = row[:, None] * inv, colp[:, None] * inv
    cos_t = jnp.concatenate([jnp.cos(ar), jnp.cos(ar), jnp.cos(ac), jnp.cos(ac)], axis=1)
    sin_t = jnp.concatenate([-jnp.sin(ar), jnp.sin(ar), -jnp.sin(ac), jnp.sin(ac)], axis=1)
    return cos_t, sin_t


def kernel(x, c, ctx, c_ctx, w_mod, b_mod, norm_g, w_in, ml_gate_b, qk_norm_g, ml_head_g,
           pool_w, pool_scale, w_up_att, w_up_ml, w_up_pool, w_out, w_ffn_in, w_ffn_out):
    cos_t, sin_t = _rope_tables()
    xa = jnp.concatenate([x.reshape(N_LAT, D), ctx.reshape(N_CTX, D)], axis=0)
    c8 = jnp.concatenate([c, c_ctx[None], jnp.zeros((8 - B - 1, D), F32)], axis=0)

    w_in_r = _regroup_w_in(w_in)
    w_out_b = w_out.astype(BF16)
    w_ffn_out_b = w_ffn_out.astype(BF16)
    w_up_att_b, w_up_ml_b, w_up_pool_b = (w.astype(BF16) for w in (w_up_att, w_up_ml, w_up_pool))
    pool_w_b = pool_w.astype(BF16)
    pool_scale_r = pool_scale.reshape(DEPTH, 1, POOL_W)
    ml_head_g_r = ml_head_g.reshape(DEPTH, 1, ML_VW)
    gate_bias = jnp.pad(jnp.transpose(ml_gate_b, (0, 2, 1, 3)).reshape(DEPTH, 1, 4 * ML_H),
                        ((0, 0), (0, 0), (0, LANES - 4 * ML_H)))

    for l in range(DEPTH):
        last = l == DEPTH - 1
        rows = N_LAT if last else N_TOK

        mod = _mod_call(c8, w_mod, b_mod, l).reshape(8, 6, 1, D)
        sh_m, sc_m, gt_m, sh_f, sc_f, gt_f = (mod[:, i] for i in range(6))

        p = _in_proj_call(xa, norm_g[l, 0][None], sh_m, sc_m, w_in_r, l)

        qt, k_proc, vt = _qkprep_call(p, cos_t, sin_t, qk_norm_g[l, 0][None], qk_norm_g[l, 1][None])
        att = _attn_call(qt, k_proc, vt, with_ctx_queries=not last)

        hf, hb = _mlstm_call(p, gate_bias[l])

        pool = _pool_call(p, pool_w_b, pool_scale_r, None, l, seq=S, first_blk=0)
        if not last:
            pool = _pool_call(p, pool_w_b, pool_scale_r, pool, l, seq=SC, first_blk=N_LAT // SC)

        xa = _merge_call(att, hf, hb, p, pool, w_up_att_b, w_up_ml_b, w_up_pool_b, w_out_b, ml_head_g_r,
                         xa, norm_g[l, 1][None], gt_m, l, rows=rows)

        f = _ffn_in_call(xa, norm_g[l, 2][None], sh_f, sc_f, w_ffn_in, l, rows=rows)
        xa = _resid_call(f, w_ffn_out_b, xa, norm_g[l, 3][None], gt_f, l, rows=rows, tm=256, name="ffn_out")

    return xa.reshape(B, S, D)
```

```python
import functools
import math

import jax
import jax.numpy as jnp
from jax import lax
from jax.experimental import pallas as pl
from jax.experimental.pallas import tpu as pltpu

F32 = jnp.float32
BF16 = jnp.bfloat16

D = 2048
B = 4
S = 4096
SC = 256
DEPTH = 2
GRID_W = 64
ATT_H = 8
ATT_KVH = 2
ATT_G = ATT_H // ATT_KVH
HD = 128
ROPE_THETA = 10000.0
ML_H = 4
ML_DK = 128
ML_DV = 256
ML_L = 128
POOL_WINDOWS = (2, 4, 8, 16)
POOL_G = 256
D_FF = 5632
EPS = 1e-6

N_LAT = B * S
N_CTX = B * SC
N_TOK = N_LAT + N_CTX

ATT_QW = ATT_H * HD
ATT_KVW = ATT_KVH * HD
ML_QKW = ML_H * ML_DK
ML_VW = ML_H * ML_DV
POOL_W = len(POOL_WINDOWS) * POOL_G

_P_GATE = 0
_P_O = 3 * D
_P_POOL = _P_O + ML_VW
_P_MLV = _P_POOL + POOL_W
_P_Q = _P_MLV + ML_VW
_P_MLK = _P_Q + ATT_QW
_P_MLQ = _P_MLK + ML_QKW
_P_K = _P_MLQ + ML_QKW
_P_V = _P_K + ATT_KVW
_P_IF = _P_V + ATT_KVW
_P_W = 12288
_P_TN = 1024
_P_SIGMOID_TILES = (_P_O + ML_VW) // _P_TN
LANES = 128

VMEM_LIMIT = 56 * 1024 * 1024


def _cparams(sem, vmem=VMEM_LIMIT):
    return pltpu.CompilerParams(dimension_semantics=sem, vmem_limit_bytes=vmem)


def _sigmoid(x):
    return 0.5 * jnp.tanh(0.5 * x) + 0.5


def _mod_kernel(c_ref, w_ref, b_ref, o_ref):
    c = c_ref[...]
    sc = (c * _sigmoid(c)).astype(BF16)
    o_ref[...] = jnp.dot(sc, w_ref[...].astype(BF16), preferred_element_type=F32) + b_ref[...]


def _mod_call(c8, w_mod, b_mod, layer):
    tn = 1024
    n = w_mod.shape[2]
    return pl.pallas_call(
        _mod_kernel,
        out_shape=jax.ShapeDtypeStruct((8, n), F32),
        grid=(n // tn,),
        in_specs=[pl.BlockSpec((8, D), lambda j: (0, 0)),
                  pl.BlockSpec((None, D, tn), lambda j: (layer, 0, j)),
                  pl.BlockSpec((None, 1, tn), lambda j: (layer, 0, j))],
        out_specs=pl.BlockSpec((8, tn), lambda j: (0, j)),
        compiler_params=_cparams(("parallel",)),
        name="mod_proj",
    )(c8, w_mod, b_mod.reshape(DEPTH, 1, n))


_ROW_CHUNK = 64


def _norm_modulate_rows(x_ref, g_ref, sh_ref, sc_ref, h_ref, tm):
    g = g_ref[...]
    one_plus = 1.0 + sc_ref[...]
    sh = sh_ref[...]

    def body(r, carry):
        rows = pl.ds(pl.multiple_of(r * _ROW_CHUNK, _ROW_CHUNK), _ROW_CHUNK)
        xf = x_ref[rows, :]
        ms = jnp.mean(xf * xf, axis=-1, keepdims=True)
        y = xf * lax.rsqrt(ms + EPS) * g
        h_ref[rows, :] = (y * one_plus + sh).astype(BF16)
        return carry

    lax.fori_loop(0, tm // _ROW_CHUNK, body, 0)


def _in_proj_kernel(*refs, tm, split_ctx):
    if split_ctx:
        x_ref, xc_ref, g_ref, sh_ref, sc_ref, w_ref, o_ref, h_scr = refs
    else:
        x_ref, g_ref, sh_ref, sc_ref, w_ref, o_ref, h_scr = refs
    i = pl.program_id(0)
    j = pl.program_id(1)
    first = j == 0
    if split_ctx:
        is_ctx = i >= N_LAT // tm

        @pl.when(first & jnp.logical_not(is_ctx))
        def _():
            _norm_modulate_rows(x_ref, g_ref, sh_ref, sc_ref, h_scr, tm)

        @pl.when(first & is_ctx)
        def _():
            _norm_modulate_rows(xc_ref, g_ref, sh_ref, sc_ref, h_scr, tm)
    else:
        @pl.when(first)
        def _():
            _norm_modulate_rows(x_ref, g_ref, sh_ref, sc_ref, h_scr, tm)

    acc = jnp.dot(h_scr[...], w_ref[...], preferred_element_type=F32)
    o_ref[...] = jnp.where(j < _P_SIGMOID_TILES, _sigmoid(acc), acc).astype(o_ref.dtype)


def _ffn_in_kernel(x_ref, g_ref, sh_ref, sc_ref, wg_ref, wu_ref, o_ref, h_scr, *, tm):
    @pl.when(pl.program_id(1) == 0)
    def _():
        _norm_modulate_rows(x_ref, g_ref, sh_ref, sc_ref, h_scr, tm)

    h = h_scr[...]
    a = jnp.dot(h, wg_ref[...].astype(BF16), preferred_element_type=F32)
    u = jnp.dot(h, wu_ref[...].astype(BF16), preferred_element_type=F32)
    o_ref[...] = (a * _sigmoid(a) * u).astype(o_ref.dtype)


def _row_specs(tm):
    return [pl.BlockSpec((tm, D), lambda i, j: (i, 0)),
            pl.BlockSpec((1, D), lambda i, j: (0, 0)),
            pl.BlockSpec((None, 1, D), lambda i, j: ((i * tm) // S, 0, 0)),
            pl.BlockSpec((None, 1, D), lambda i, j: ((i * tm) // S, 0, 0))]


def _in_proj_call(x, x_ctx, g, shift, scale, w, layer):
    tm, tn = 1024, _P_TN
    assert N_CTX == tm
    specs = _row_specs(tm)
    args = [x, g, shift, scale, w]
    if x_ctx is not None:
        last_lat = N_LAT // tm - 1
        specs[0] = pl.BlockSpec((tm, D), lambda i, j: (jnp.minimum(i, last_lat), 0))
        specs.insert(1, pl.BlockSpec((tm, D), lambda i, j: (0, 0), pipeline_mode=pl.Buffered(1)))
        args.insert(1, x_ctx)
    return pl.pallas_call(
        functools.partial(_in_proj_kernel, tm=tm, split_ctx=x_ctx is not None),
        out_shape=jax.ShapeDtypeStruct((N_TOK, _P_W), BF16),
        grid=(N_TOK // tm, _P_W // tn),
        in_specs=specs + [pl.BlockSpec((None, D, tn), lambda i, j: (layer, 0, j))],
        out_specs=pl.BlockSpec((tm, tn), lambda i, j: (i, j)),
        scratch_shapes=[pltpu.VMEM((tm, D), BF16)],
        compiler_params=_cparams(("parallel", "arbitrary")),
        name="in_proj",
    )(*args)


def _ffn_in_call(x, g, shift, scale, w, layer, *, rows):
    tm, tn = 1024, 512
    nj = D_FF // tn
    return pl.pallas_call(
        functools.partial(_ffn_in_kernel, tm=tm),
        out_shape=jax.ShapeDtypeStruct((rows, D_FF), BF16),
        grid=(rows // tm, nj),
        in_specs=_row_specs(tm) + [pl.BlockSpec((None, D, tn), lambda i, j: (layer, 0, j)),
                                   pl.BlockSpec((None, D, tn), lambda i, j: (layer, 0, nj + j))],
        out_specs=pl.BlockSpec((tm, tn), lambda i, j: (i, j)),
        scratch_shapes=[pltpu.VMEM((tm, D), BF16)],
        compiler_params=_cparams(("parallel", "arbitrary")),
        name="ffn_in",
    )(x, g, shift, scale, w, w)


def _resid_kernel(a_ref, w_ref, x_ref, g_ref, gate_ref, o_ref):
    z = jnp.dot(a_ref[...], w_ref[...], preferred_element_type=F32)
    ms = jnp.mean(z * z, axis=-1, keepdims=True)
    o_ref[...] = x_ref[...] + gate_ref[...] * (z * lax.rsqrt(ms + EPS) * g_ref[...])


def _resid_call(a, w, x, g, gate, layer, *, rows, tm, name):
    k = w.shape[1]
    return pl.pallas_call(
        _resid_kernel,
        out_shape=jax.ShapeDtypeStruct((rows, D), F32),
        grid=(rows // tm,),
        in_specs=[pl.BlockSpec((tm, k), lambda i: (i, 0)),
                  pl.BlockSpec((None, k, D), lambda i: (layer, 0, 0), pipeline_mode=pl.Buffered(1)),
                  pl.BlockSpec((tm, D), lambda i: (i, 0)),
                  pl.BlockSpec((1, D), lambda i: (0, 0)),
                  pl.BlockSpec((None, 1, D), lambda i: ((i * tm) // S, 0, 0))],
        out_specs=pl.BlockSpec((tm, D), lambda i: (i, 0)),
        compiler_params=_cparams(("parallel",)),
        name=name,
    )(a, w, x, g, gate)


def _head_norm_rope(xh, g, cos, sin, first_half):
    xf = xh.astype(F32)
    ms = jnp.mean(xf * xf, axis=-1, keepdims=True)
    y = xf * lax.rsqrt(ms + EPS) * g
    swapped = jnp.where(first_half, pltpu.roll(y, HD - HD // 4, 1), pltpu.roll(y, HD // 4, 1))
    return y * cos + swapped * sin


def _qkprep_kernel(q_ref, k_ref, v_ref, cos_ref, sin_ref, gq_ref, gk_ref, qt_ref, ko_ref, vt_ref,
                   *, lat_tiles):
    use_rope = pl.program_id(0) < lat_tiles
    cos = jnp.where(use_rope, cos_ref[...], 1.0)
    sin = jnp.where(use_rope, sin_ref[...], 0.0)
    lane = lax.broadcasted_iota(jnp.int32, (1, HD), 1)
    first_half = (lane % (HD // 2)) < (HD // 4)
    q_scale = HD ** -0.5 * math.log2(math.e)
    for h in range(ATT_H):
        cs = slice(h * HD, (h + 1) * HD)
        y = _head_norm_rope(q_ref[:, cs], gq_ref[...], cos, sin, first_half)
        qt_ref[cs, :] = (y * q_scale).T.astype(BF16)
    for h in range(ATT_KVH):
        cs = slice(h * HD, (h + 1) * HD)
        y = _head_norm_rope(k_ref[:, cs], gk_ref[...], cos, sin, first_half)
        ko_ref[:, cs] = y.astype(BF16)
        vt_ref[cs, :] = v_ref[:, cs].astype(F32).T.astype(BF16)


def _qkprep_call(p, cos_t, sin_t, gq, gk):
    tm = 512
    lat_tiles = N_LAT // tm
    pos_tiles = S // tm
    return pl.pallas_call(
        functools.partial(_qkprep_kernel, lat_tiles=lat_tiles),
        out_shape=(jax.ShapeDtypeStruct((ATT_QW, N_TOK), BF16),
                   jax.ShapeDtypeStruct((N_TOK, ATT_KVW), BF16),
                   jax.ShapeDtypeStruct((ATT_KVW, N_TOK), BF16)),
        grid=(N_TOK // tm,),
        in_specs=[pl.BlockSpec((tm, ATT_QW), lambda i: (i, _P_Q // ATT_QW)),
                  pl.BlockSpec((tm, ATT_KVW), lambda i: (i, _P_K // ATT_KVW)),
                  pl.BlockSpec((tm, ATT_KVW), lambda i: (i, _P_V // ATT_KVW)),
                  pl.BlockSpec((tm, HD), lambda i: (i % pos_tiles, 0)),
                  pl.BlockSpec((tm, HD), lambda i: (i % pos_tiles, 0)),
                  pl.BlockSpec((1, HD), lambda i: (0, 0)),
                  pl.BlockSpec((1, HD), lambda i: (0, 0))],
        out_specs=(pl.BlockSpec((ATT_QW, tm), lambda i: (0, i)),
                   pl.BlockSpec((tm, ATT_KVW), lambda i: (i, 0)),
                   pl.BlockSpec((ATT_KVW, tm), lambda i: (0, i))),
        compiler_params=_cparams(("parallel",)),
        name="qk_prep",
    )(p, p, p, cos_t, sin_t, gq, gk)


_TQ = 256
_KC = 512


def _attention_heads(qt_ref, key_refs, vt_refs, o_ref, s_scr, p_scr):
    chunks = []
    spans = []
    base = 0
    for k_ref in key_refs:
        n_keys = k_ref.shape[0]
        kc = min(_KC, n_keys)
        chunks += [(k_ref, c * kc, kc, base + c * kc) for c in range(n_keys // kc)]
        spans.append((base, n_keys))
        base += n_keys

    def scores(h):
        slot = h % 2
        qt = qt_ref[h * HD:(h + 1) * HD, :]
        mx = None
        for k_ref, r0, nr, s0 in chunks:
            sc = jnp.dot(k_ref[r0:r0 + nr, :], qt, preferred_element_type=F32)
            s_scr[slot, s0:s0 + nr, :] = sc
            part = jnp.max(sc.reshape(nr // 8, 8, _TQ), axis=0)
            mx = part if mx is None else jnp.maximum(mx, part)
        return jnp.max(mx, axis=0, keepdims=True)

    def probs(h, m):
        slot = h % 2
        tot = None
        for _, _, nr, s0 in chunks:
            pr = jnp.exp2(s_scr[slot, s0:s0 + nr, :] - m)
            p_scr[slot, s0:s0 + nr, :] = pr.astype(BF16)
            part = jnp.sum(pr.reshape(nr // 8, 8, _TQ), axis=0)
            tot = part if tot is None else tot + part
        return jnp.sum(tot, axis=0, keepdims=True)

    def output(h, denom):
        slot = h % 2
        ot = None
        for vt_ref, (s0, nr) in zip(vt_refs, spans):
            part = jnp.dot(vt_ref[...], p_scr[slot, s0:s0 + nr, :], preferred_element_type=F32)
            ot = part if ot is None else ot + part
        o_ref[:, h * HD:(h + 1) * HD] = (ot * (1.0 / denom)).T.astype(o_ref.dtype)

    m = scores(0)
    for h in range(ATT_G):
        m_next = scores(h + 1) if h + 1 < ATT_G else None
        output(h, probs(h, m))
        m = m_next


def _attn_latent_kernel(qt_ref, kl_ref, kc_ref, vtl_ref, vtc_ref, o_ref, s_scr, p_scr):
    _attention_heads(qt_ref, (kl_ref, kc_ref), (vtl_ref, vtc_ref), o_ref, s_scr, p_scr)


def _attn_ctx_kernel(qt_ref, kc_ref, vtc_ref, o_ref, s_scr, p_scr):
    _attention_heads(qt_ref, (kc_ref,), (vtc_ref,), o_ref, s_scr, p_scr)


def _attn_call(qt, k_proc, vt, *, with_ctx_queries):
    lat_tiles = S // _TQ
    ctx_blk = N_LAT // SC
    gw = ATT_G * HD

    att = pl.pallas_call(
        _attn_latent_kernel,
        out_shape=jax.ShapeDtypeStruct((N_LAT, ATT_QW), BF16),
        grid=(B, ATT_KVH, lat_tiles),
        in_specs=[pl.BlockSpec((gw, _TQ), lambda b, kv, t: (kv, b * lat_tiles + t)),
                  pl.BlockSpec((S, HD), lambda b, kv, t: (b, kv)),
                  pl.BlockSpec((SC, HD), lambda b, kv, t: (ctx_blk + b, kv)),
                  pl.BlockSpec((HD, S), lambda b, kv, t: (kv, b)),
                  pl.BlockSpec((HD, SC), lambda b, kv, t: (kv, ctx_blk + b))],
        out_specs=pl.BlockSpec((_TQ, gw), lambda b, kv, t: (b * lat_tiles + t, kv)),
        scratch_shapes=[pltpu.VMEM((2, S + SC, _TQ), F32),
                        pltpu.VMEM((2, S + SC, _TQ), BF16)],
        compiler_params=_cparams(("parallel", "parallel", "arbitrary")),
        name="gqa_attention",
    )(qt, k_proc, k_proc, vt, vt)
    if not with_ctx_queries:
        return att, None
    att_ctx = pl.pallas_call(
        _attn_ctx_kernel,
        out_shape=jax.ShapeDtypeStruct((N_CTX, ATT_QW), BF16),
        grid=(B, ATT_KVH),
        in_specs=[pl.BlockSpec((gw, SC), lambda b, kv: (kv, ctx_blk + b)),
                  pl.BlockSpec((SC, HD), lambda b, kv: (ctx_blk + b, kv)),
                  pl.BlockSpec((HD, SC), lambda b, kv: (kv, ctx_blk + b))],
        out_specs=pl.BlockSpec((SC, gw), lambda b, kv: (b, kv)),
        scratch_shapes=[pltpu.VMEM((2, SC, _TQ), F32),
                        pltpu.VMEM((2, SC, _TQ), BF16)],
        compiler_params=_cparams(("parallel", "parallel")),
        name="gqa_attention_ctx",
    )(qt, k_proc, vt)
    return att, att_ctx


_ML_CHUNKS = (SC + S) // ML_L
_ML_CTX_CHUNKS = SC // ML_L
_NEG = -1e30


def _log_sigmoid(x):
    return jnp.minimum(x, 0.0) - jnp.log1p(jnp.exp(-jnp.abs(x)))


def _exact_dot_f32(a01, x):
    hi = x.astype(BF16)
    r1 = x - hi.astype(F32)
    mid = r1.astype(BF16)
    lo = (r1 - mid.astype(F32)).astype(BF16)
    return (jnp.dot(a01, hi, preferred_element_type=F32)
            + jnp.dot(a01, mid, preferred_element_type=F32)
            + jnp.dot(a01, lo, preferred_element_type=F32))


def _gate_lane(kind, d, h):
    return kind * 2 * ML_H + d * ML_H + h


def _mlstm_kernel(qf_ref, kf_ref, vf_ref, gf_ref, qb_ref, kb_ref, vb_ref, gb_ref, bias_ref,
                  hf_ref, hb_ref, ct_scr, n_scr, m_scr):
    @pl.when(pl.program_id(1) == 0)
    def _():
        ct_scr[...] = jnp.zeros_like(ct_scr)
        n_scr[...] = jnp.zeros_like(n_scr)
        m_scr[...] = jnp.zeros_like(m_scr)

    L = ML_L
    row = lax.broadcasted_iota(jnp.int32, (L, L), 0)
    col = lax.broadcasted_iota(jnp.int32, (L, L), 1)
    q_scale = ML_DK ** -0.5

    for d, (q_ref, k_ref, v_ref, g_ref, h_ref) in enumerate(
            ((qf_ref, kf_ref, vf_ref, gf_ref, hf_ref), (qb_ref, kb_ref, vb_ref, gb_ref, hb_ref))):
        mask = (col <= row) if d == 0 else (col >= row)
        tri = jnp.where(mask, 1.0, 0.0).astype(BF16)
        gates = g_ref[...].astype(F32) + bias_ref[...]
        log_f = _log_sigmoid(gates)
        cum_lf = _exact_dot_f32(tri, log_f)
        gates_t = gates.T
        cum_lf_t = cum_lf.T
        tot_lf = jnp.sum(log_f, axis=0, keepdims=True)
        for h in range(ML_H):
            idx = d * ML_H + h
            li_lane = _gate_lane(0, d, h)
            lf_lane = _gate_lane(1, d, h)
            b_c = cum_lf[:, lf_lane:lf_lane + 1]
            b_r = cum_lf_t[lf_lane:lf_lane + 1, :]
            li_c = gates[:, li_lane:li_lane + 1]
            li_r = gates_t[li_lane:li_lane + 1, :]
            g_tot = tot_lf[:, lf_lane:lf_lane + 1]
            m_prev = m_scr[idx][0:1, 0:1]
            n_prev = n_scr[idx][0:1, :]
            ct_prev = ct_scr[idx]

            q = (q_ref[:, h * ML_DK:(h + 1) * ML_DK].astype(F32) * q_scale).astype(BF16)
            k = k_ref[:, h * ML_DK:(h + 1) * ML_DK]
            v = v_ref[:, h * ML_DV:(h + 1) * ML_DV]

            qk = lax.dot_general(q, k, (((1,), (1,)), ((), ())), preferred_element_type=F32)
            d_log = jnp.where(mask, b_c - b_r + li_r, _NEG)
            a_log = b_c + m_prev
            m_j = jnp.maximum(a_log, jnp.max(d_log, axis=-1, keepdims=True))
            w = jnp.exp(d_log - m_j) * qk
            inter = jnp.exp(a_log - m_j)
            num = (jnp.dot(w.astype(BF16), v, preferred_element_type=F32)
                   + inter * jnp.dot(q, ct_prev.astype(BF16), preferred_element_type=F32))
            nq = jnp.sum(q.astype(F32) * n_prev, axis=-1, keepdims=True)
            den = jnp.sum(w, axis=-1, keepdims=True) + inter * nq
            h_out = num / jnp.maximum(jnp.abs(den), jnp.exp(-m_j))
            h_ref[:, h * ML_DV:(h + 1) * ML_DV] = h_out.astype(h_ref.dtype)

            w_end = g_tot - b_c + li_c
            m_loc = jnp.max(w_end, axis=0, keepdims=True)
            e_c = jnp.exp(w_end - m_loc)
            m_new = jnp.maximum(g_tot + m_prev, m_loc)
            a_old = jnp.exp(g_tot + m_prev - m_new)
            a_loc = jnp.exp(m_loc - m_new)
            ev = (e_c * v.astype(F32)).astype(BF16)
            c_loc = lax.dot_general(k, ev, (((0,), (0,)), ((), ())), preferred_element_type=F32)
            n_loc = jnp.sum(e_c * k.astype(F32), axis=0, keepdims=True)
            ct_scr[idx] = a_old * ct_prev + a_loc * c_loc
            n_scr[idx] = jnp.broadcast_to(a_old * n_prev + a_loc * n_loc, (8, ML_DK))
            m_scr[idx] = jnp.broadcast_to(m_new, (8, LANES))


def _mlstm_call(p, gate_bias):
    lat_chunks = S // ML_L
    ctx_blk = N_LAT // ML_L

    def fwd_rb(b, t):
        return jnp.where(t < _ML_CTX_CHUNKS, ctx_blk + b * _ML_CTX_CHUNKS + t,
                         b * lat_chunks + t - _ML_CTX_CHUNKS)

    def rev_rb(b, t):
        return jnp.where(t < _ML_CTX_CHUNKS, ctx_blk + b * _ML_CTX_CHUNKS + (_ML_CTX_CHUNKS - 1 - t),
                         b * lat_chunks + (_ML_CHUNKS - 1 - t))

    def specs(rb):
        return [pl.BlockSpec((ML_L, ML_QKW), lambda b, t: (rb(b, t), _P_MLQ // ML_QKW)),
                pl.BlockSpec((ML_L, ML_QKW), lambda b, t: (rb(b, t), _P_MLK // ML_QKW)),
                pl.BlockSpec((ML_L, ML_VW), lambda b, t: (rb(b, t), _P_MLV // ML_VW)),
                pl.BlockSpec((ML_L, LANES), lambda b, t: (rb(b, t), _P_IF // LANES))]

    n_chain = 2 * ML_H
    return pl.pallas_call(
        _mlstm_kernel,
        out_shape=(jax.ShapeDtypeStruct((N_TOK, ML_VW), BF16),
                   jax.ShapeDtypeStruct((N_TOK, ML_VW), BF16)),
        grid=(B, _ML_CHUNKS),
        in_specs=specs(fwd_rb) + specs(rev_rb) + [pl.BlockSpec((1, LANES), lambda b, t: (0, 0))],
        out_specs=(pl.BlockSpec((ML_L, ML_VW), lambda b, t: (fwd_rb(b, t), 0)),
                   pl.BlockSpec((ML_L, ML_VW), lambda b, t: (rev_rb(b, t), 0))),
        scratch_shapes=[pltpu.VMEM((n_chain, ML_DK, ML_DV), F32),
                        pltpu.VMEM((n_chain, 8, ML_DK), F32),
                        pltpu.VMEM((n_chain, 8, LANES), F32)],
        compiler_params=_cparams(("parallel", "arbitrary")),
        name="mlstm",
    )(p, p, p, p, p, p, p, p, gate_bias)


def _pool_kernel(u_ref, w_ref, sc_ref, o_ref, mixed_scr, *, seq):
    gi = pl.program_id(1)
    t = lax.broadcasted_iota(jnp.int32, (seq, 1), 0)

    def prev(x, k):
        return jnp.where(t >= k, pltpu.roll(x, k, 0), 0.0)

    def nxt(x, k):
        return jnp.where(t < seq - k, pltpu.roll(x, seq - k, 0), 0.0)

    half_w = jnp.where(gi == 0, 1, jnp.where(gi == 1, 2, jnp.where(gi == 2, 4, 8)))
    count = (jnp.minimum(t + half_w, seq) - jnp.maximum(t - half_w, 0)).astype(F32)
    for c in range(POOL_G // LANES):
        cs = slice(c * LANES, (c + 1) * LANES)
        u = u_ref[:, cs].astype(F32)
        before = prev(u, 1)
        after = u
        win = before + after
        for step, k in enumerate((1, 2, 4)):
            before = before + prev(before, k)
            after = after + nxt(after, k)
            win = jnp.where(gi == step + 1, before + after, win)
        mixed_scr[:, cs] = (win / count - u).astype(BF16)
    o_ref[...] = (jnp.dot(mixed_scr[...], w_ref[...], preferred_element_type=F32)
                  * sc_ref[...]).astype(o_ref.dtype)


def _pool_call(p, pool_w, pool_scale, layer, *, seq, first_blk):
    n_groups = len(POOL_WINDOWS)
    return pl.pallas_call(
        functools.partial(_pool_kernel, seq=seq),
        out_shape=jax.ShapeDtypeStruct((B * seq, POOL_W), BF16),
        grid=(B, n_groups),
        in_specs=[pl.BlockSpec((seq, POOL_G), lambda b, g: (first_blk + b, _P_POOL // POOL_G + g)),
                  pl.BlockSpec((None, None, POOL_G, POOL_G), lambda b, g: (layer, g, 0, 0)),
                  pl.BlockSpec((None, 1, POOL_G), lambda b, g: (layer, 0, g))],
        out_specs=pl.BlockSpec((seq, POOL_G), lambda b, g: (b, g)),
        scratch_shapes=[pltpu.VMEM((seq, POOL_G), BF16)],
        compiler_params=_cparams(("parallel", "parallel")),
        name="pool_branch_%d" % seq,
    )(p, pool_w, pool_scale)


def _merge_kernel(*refs, split_ctx, lat_blocks):
    if split_ctx:
        (att_ref, attc_ref, hf_ref, hb_ref, og_ref, pool_ref, poolc_ref, g0_ref, g1_ref, g2_ref,
         wa_ref, wm_ref, wp_ref, wo_ref, mlg_ref, x_ref, xc_ref, ng_ref, gate_ref, o_ref, ml_scr) = refs
        is_ctx = pl.program_id(0) >= lat_blocks
        att = jnp.where(is_ctx, attc_ref[...], att_ref[...])
        pool = jnp.where(is_ctx, poolc_ref[...], pool_ref[...])
        x = jnp.where(is_ctx, xc_ref[...], x_ref[...])
    else:
        (att_ref, hf_ref, hb_ref, og_ref, pool_ref, g0_ref, g1_ref, g2_ref,
         wa_ref, wm_ref, wp_ref, wo_ref, mlg_ref, x_ref, ng_ref, gate_ref, o_ref, ml_scr) = refs
        att, pool, x = att_ref[...], pool_ref[...], x_ref[...]

    for h in range(ML_H):
        cs = slice(h * ML_DV, (h + 1) * ML_DV)
        hs = hf_ref[:, cs].astype(F32) + hb_ref[:, cs].astype(F32)
        ms = jnp.mean(hs * hs, axis=-1, keepdims=True)
        normed = hs * lax.rsqrt(ms + EPS) * mlg_ref[:, cs]
        ml_scr[:, cs] = (normed * og_ref[:, cs].astype(F32)).astype(BF16)

    def branch(g_ref, a, w_ref):
        return g_ref[...].astype(F32) * jnp.dot(a, w_ref[...], preferred_element_type=F32)

    y = branch(g0_ref, att, wa_ref) + branch(g1_ref, ml_scr[...], wm_ref) + branch(g2_ref, pool, wp_ref)
    z = jnp.dot(y.astype(BF16), wo_ref[...], preferred_element_type=F32)
    ms = jnp.mean(z * z, axis=-1, keepdims=True)
    o_ref[...] = x + gate_ref[...] * (z * lax.rsqrt(ms + EPS) * ng_ref[...])


def _merge_call(att, att_ctx, hf, hb, p, pool, pool_ctx, w_up_att, w_up_ml, w_up_pool, w_out, ml_head_g,
                x, x_ctx, ng, gate, layer):
    tm = 256
    split_ctx = att_ctx is not None
    rows = N_TOK if split_ctx else N_LAT
    lat_blocks = N_LAT // tm

    def row_blk(width, col=0):
        return pl.BlockSpec((tm, width), lambda i: (i, col))

    def lat_blk(width):
        if not split_ctx:
            return row_blk(width)
        return pl.BlockSpec((tm, width), lambda i: (jnp.minimum(i, lat_blocks - 1), 0))

    def ctx_blk(width):
        return pl.BlockSpec((tm, width), lambda i: (jnp.maximum(i - lat_blocks, 0), 0))

    def w_blk(k):
        return pl.BlockSpec((None, k, D), lambda i: (layer, 0, 0), pipeline_mode=pl.Buffered(1))

    def with_ctx(spec_arg, ctx_spec_arg):
        return [spec_arg, ctx_spec_arg] if split_ctx else [spec_arg]

    specs_args = (
        with_ctx((lat_blk(ATT_QW), att), (ctx_blk(ATT_QW), att_ctx))
        + [(row_blk(ML_VW), hf), (row_blk(ML_VW), hb), (row_blk(ML_VW, _P_O // ML_VW), p)]
        + with_ctx((lat_blk(POOL_W), pool), (ctx_blk(POOL_W), pool_ctx))
        + [(row_blk(D, _P_GATE // D + branch), p) for branch in range(3)]
        + [(w_blk(ATT_QW), w_up_att), (w_blk(ML_VW), w_up_ml), (w_blk(POOL_W), w_up_pool), (w_blk(D), w_out),
           (pl.BlockSpec((None, 1, ML_VW), lambda i: (layer, 0, 0)), ml_head_g)]
        + with_ctx((lat_blk(D), x), (ctx_blk(D), x_ctx))
        + [(pl.BlockSpec((1, D), lambda i: (0, 0)), ng),
           (pl.BlockSpec((None, 1, D), lambda i: ((i * tm) // S, 0, 0)), gate)])

    return pl.pallas_call(
        functools.partial(_merge_kernel, split_ctx=split_ctx, lat_blocks=lat_blocks),
        out_shape=jax.ShapeDtypeStruct((rows, D), F32),
        grid=(rows // tm,),
        in_specs=[sa[0] for sa in specs_args],
        out_specs=pl.BlockSpec((tm, D), lambda i: (i, 0)),
        scratch_shapes=[pltpu.VMEM((tm, ML_VW), BF16)],
        compiler_params=_cparams(("parallel",)),
        name="merge_out",
    )(*[sa[1] for sa in specs_args])


_N_IF = 4 * ML_H
_W_IN_COLS = 2 * ATT_KVW + ML_QKW + ML_VW + _N_IF + ATT_QW + ML_QKW + ML_VW + POOL_W + 3 * D
_W_IN_MAIN = (_W_IN_COLS // LANES) * LANES
_W_IN_IF = 2 * ATT_KVW + ML_QKW + ML_VW
_W_IN_SEGMENTS = (
    (0, ATT_KVW, _P_K),
    (ATT_KVW, ATT_KVW, _P_V),
    (2 * ATT_KVW, ML_QKW, _P_MLK),
    (2 * ATT_KVW + ML_QKW, ML_VW, _P_MLV),
    (_W_IN_IF + _N_IF, ATT_QW, _P_Q),
    (_W_IN_IF + _N_IF + ATT_QW, ML_QKW, _P_MLQ),
    (_W_IN_IF + _N_IF + ATT_QW + ML_QKW, ML_VW, _P_O),
    (_W_IN_IF + _N_IF + ATT_QW + ML_QKW + ML_VW, POOL_W, _P_POOL),
    (_W_IN_IF + _N_IF + ATT_QW + ML_QKW + ML_VW + POOL_W, 3 * D, _P_GATE),
)
_RG_TILE = 256


def _regroup_kernel(w_ref, tail_ref, o_ref):
    rows = w_ref.shape[0]
    lane = lax.broadcasted_iota(jnp.int32, (rows, LANES), 1)
    win_w = _RG_TILE + LANES
    src_i = lax.broadcasted_iota(jnp.int32, (win_w, _RG_TILE), 0)
    dst_i = lax.broadcasted_iota(jnp.int32, (win_w, _RG_TILE), 1)
    select = jnp.where(src_i == dst_i + _N_IF, 1.0, 0.0).astype(BF16)
    tail = jnp.where(lane < _W_IN_COLS - _W_IN_MAIN, tail_ref[...], 0.0).astype(BF16)

    for src, width, dst in _W_IN_SEGMENTS:
        if src % LANES == 0:
            o_ref[:, dst:dst + width] = w_ref[:, src:src + width].astype(BF16)
            continue
        assert src % LANES == _N_IF and width % _RG_TILE == 0
        for c in range(0, width, _RG_TILE):
            a0 = src + c - _N_IF
            if a0 + win_w <= _W_IN_MAIN:
                win = w_ref[:, a0:a0 + win_w].astype(BF16)
            else:
                assert a0 + _RG_TILE == _W_IN_MAIN
                win = jnp.concatenate([w_ref[:, a0:a0 + _RG_TILE].astype(BF16), tail], axis=1)
            o_ref[:, dst + c:dst + c + _RG_TILE] = jnp.dot(
                win, select, preferred_element_type=F32).astype(BF16)

    gate_cols = jnp.where(lane < _N_IF, w_ref[:, _W_IN_IF:_W_IN_IF + LANES], 0.0)
    o_ref[:, _P_IF:_P_IF + LANES] = gate_cols.astype(BF16)
    o_ref[:, _P_IF + LANES:] = jnp.zeros((rows, _P_W - _P_IF - LANES), BF16)


def _regroup_call(w_in):
    rows = 256
    return pl.pallas_call(
        _regroup_kernel,
        out_shape=jax.ShapeDtypeStruct((DEPTH, D, _P_W), BF16),
        grid=(DEPTH, D // rows),
        in_specs=[pl.BlockSpec((None, rows, _W_IN_MAIN), lambda l, r: (l, r, 0)),
                  pl.BlockSpec((None, rows, LANES), lambda l, r: (l, r, _W_IN_MAIN // LANES))],
        out_specs=pl.BlockSpec((None, rows, _P_W), lambda l, r: (l, r, 0)),
        compiler_params=_cparams(("parallel", "parallel")),
        name="w_in_regroup",
    )(w_in, w_in)


def _rope_tables():
    t = jnp.arange(S)
    row = (t // GRID_W).astype(F32)
    colp = (t % GRID_W).astype(F32)
    n_freq = HD // 4
    inv = ROPE_THETA ** (-jnp.arange(n_freq, dtype=F32) / n_freq)
    ar, ac = row[:, None] * inv, colp[:, None] * inv
    cos_t = jnp.concatenate([jnp.cos(ar), jnp.cos(ar), jnp.cos(ac), jnp.cos(ac)], axis=1)
    sin_t = jnp.concatenate([-jnp.sin(ar), jnp.sin(ar), -jnp.sin(ac), jnp.sin(ac)], axis=1)
    return cos_t, sin_t


def kernel(x, c, ctx, c_ctx, w_mod, b_mod, norm_g, w_in, ml_gate_b, qk_norm_g, ml_head_g,
           pool_w, pool_scale, w_up_att, w_up_ml, w_up_pool, w_out, w_ffn_in, w_ffn_out):
    cos_t, sin_t = _rope_tables()
    x_lat = x.reshape(N_LAT, D)
    x_ctx = ctx.reshape(N_CTX, D)
    c8 = jnp.concatenate([c, c_ctx[None], jnp.zeros((8 - B - 1, D), F32)], axis=0)

    w_in_r = _regroup_call(w_in)
    w_out_b = w_out.astype(BF16)
    w_ffn_out_b = w_ffn_out.astype(BF16)
    w_up_att_b, w_up_ml_b, w_up_pool_b = (w.astype(BF16) for w in (w_up_att, w_up_ml, w_up_pool))
    pool_w_b = pool_w.astype(BF16)
    pool_scale_r = pool_scale.reshape(DEPTH, 1, POOL_W)
    ml_head_g_r = ml_head_g.reshape(DEPTH, 1, ML_VW)
    gate_bias = jnp.pad(jnp.transpose(ml_gate_b, (0, 2, 1, 3)).reshape(DEPTH, 1, _N_IF),
                        ((0, 0), (0, 0), (0, LANES - _N_IF)))

    assert DEPTH == 2
    xa = None
    for l in range(DEPTH):
        last = l == DEPTH - 1
        rows = N_LAT if last else N_TOK

        mod = _mod_call(c8, w_mod, b_mod, l).reshape(8, 6, 1, D)
        sh_m, sc_m, gt_m, sh_f, sc_f, gt_f = (mod[:, i] for i in range(6))

        if l == 0:
            p = _in_proj_call(x_lat, x_ctx, norm_g[l, 0][None], sh_m, sc_m, w_in_r, l)
        else:
            p = _in_proj_call(xa, None, norm_g[l, 0][None], sh_m, sc_m, w_in_r, l)

        qt, k_proc, vt = _qkprep_call(p, cos_t, sin_t, qk_norm_g[l, 0][None], qk_norm_g[l, 1][None])
        att, att_ctx = _attn_call(qt, k_proc, vt, with_ctx_queries=not last)

        hf, hb = _mlstm_call(p, gate_bias[l])

        pool = _pool_call(p, pool_w_b, pool_scale_r, l, seq=S, first_blk=0)
        pool_ctx = None if last else _pool_call(p, pool_w_b, pool_scale_r, l, seq=SC, first_blk=N_LAT // SC)

        xa = _merge_call(att, att_ctx, hf, hb, p, pool, pool_ctx, w_up_att_b, w_up_ml_b, w_up_pool_b,
                         w_out_b, ml_head_g_r, x_lat if l == 0 else xa, x_ctx if l == 0 else None,
                         norm_g[l, 1][None], gt_m, l)

        f = _ffn_in_call(xa, norm_g[l, 2][None], sh_f, sc_f, w_ffn_in, l, rows=rows)
        xa = _resid_call(f, w_ffn_out_b, xa, norm_g[l, 3][None], gt_f, l, rows=rows, tm=256, name="ffn_out")

    return xa.reshape(B, S, D)
```

```python
import functools
import math

import jax
import jax.numpy as jnp
from jax import lax
from jax.experimental import pallas as pl
from jax.experimental.pallas import tpu as pltpu

F32 = jnp.float32
BF16 = jnp.bfloat16

D = 2048
B = 4
S = 4096
SC = 256
DEPTH = 2
GRID_W = 64
ATT_H = 8
ATT_KVH = 2
ATT_G = ATT_H // ATT_KVH
HD = 128
ROPE_THETA = 10000.0
ML_H = 4
ML_DK = 128
ML_DV = 256
ML_L = 128
POOL_WINDOWS = (2, 4, 8, 16)
POOL_G = 256
D_FF = 5632
EPS = 1e-6

N_LAT = B * S
N_CTX = B * SC
N_TOK = N_LAT + N_CTX

ATT_QW = ATT_H * HD
ATT_KVW = ATT_KVH * HD
ML_QKW = ML_H * ML_DK
ML_VW = ML_H * ML_DV
POOL_W = len(POOL_WINDOWS) * POOL_G

_P_GATE = 0
_P_O = 3 * D
_P_POOL = _P_O + ML_VW
_P_MLV = _P_POOL + POOL_W
_P_Q = _P_MLV + ML_VW
_P_MLK = _P_Q + ATT_QW
_P_MLQ = _P_MLK + ML_QKW
_P_K = _P_MLQ + ML_QKW
_P_V = _P_K + ATT_KVW
_P_IF = _P_V + ATT_KVW
_P_W = 12288
_P_TN = 1024
_P_SIGMOID_TILES = (_P_O + ML_VW) // _P_TN
LANES = 128

VMEM_LIMIT = 56 * 1024 * 1024


def _cparams(sem, vmem=VMEM_LIMIT):
    return pltpu.CompilerParams(dimension_semantics=sem, vmem_limit_bytes=vmem)


def _sigmoid(x):
    return 0.5 * jnp.tanh(0.5 * x) + 0.5


def _mod_kernel(c_ref, w_ref, b_ref, o_ref):
    c = c_ref[...]
    sc = (c * _sigmoid(c)).astype(BF16)
    o_ref[...] = jnp.dot(sc, w_ref[...].astype(BF16), preferred_element_type=F32) + b_ref[...]


def _mod_call(c8, w_mod, b_mod, layer):
    tn = 1024
    n = w_mod.shape[2]
    return pl.pallas_call(
        _mod_kernel,
        out_shape=jax.ShapeDtypeStruct((8, n), F32),
        grid=(n // tn,),
        in_specs=[pl.BlockSpec((8, D), lambda j: (0, 0)),
                  pl.BlockSpec((None, D, tn), lambda j: (layer, 0, j)),
                  pl.BlockSpec((None, 1, tn), lambda j: (layer, 0, j))],
        out_specs=pl.BlockSpec((8, tn), lambda j: (0, j)),
        compiler_params=_cparams(("parallel",)),
        name="mod_proj",
    )(c8, w_mod, b_mod.reshape(DEPTH, 1, n))


_ROW_CHUNK = 64


def _norm_modulate_rows(x_ref, g_ref, sh_ref, sc_ref, h_ref, tm):
    g = g_ref[...]
    one_plus = 1.0 + sc_ref[...]
    sh = sh_ref[...]

    def body(r, carry):
        rows = pl.ds(pl.multiple_of(r * _ROW_CHUNK, _ROW_CHUNK), _ROW_CHUNK)
        xf = x_ref[rows, :]
        ms = jnp.mean(xf * xf, axis=-1, keepdims=True)
        y = xf * lax.rsqrt(ms + EPS) * g
        h_ref[rows, :] = (y * one_plus + sh).astype(BF16)
        return carry

    lax.fori_loop(0, tm // _ROW_CHUNK, body, 0)


def _in_proj_kernel(*refs, tm, split_ctx):
    if split_ctx:
        x_ref, xc_ref, g_ref, sh_ref, sc_ref, w_ref, o_ref, h_scr = refs
    else:
        x_ref, g_ref, sh_ref, sc_ref, w_ref, o_ref, h_scr = refs
    i = pl.program_id(0)
    j = pl.program_id(1)
    first = j == 0
    if split_ctx:
        is_ctx = i >= N_LAT // tm

        @pl.when(first & jnp.logical_not(is_ctx))
        def _():
            _norm_modulate_rows(x_ref, g_ref, sh_ref, sc_ref, h_scr, tm)

        @pl.when(first & is_ctx)
        def _():
            _norm_modulate_rows(xc_ref, g_ref, sh_ref, sc_ref, h_scr, tm)
    else:
        @pl.when(first)
        def _():
            _norm_modulate_rows(x_ref, g_ref, sh_ref, sc_ref, h_scr, tm)

    acc = jnp.dot(h_scr[...], w_ref[...], preferred_element_type=F32)
    o_ref[...] = jnp.where(j < _P_SIGMOID_TILES, _sigmoid(acc), acc).astype(o_ref.dtype)


def _ffn_in_kernel(x_ref, g_ref, sh_ref, sc_ref, wg_ref, wu_ref, o_ref, h_scr, *, tm):
    @pl.when(pl.program_id(1) == 0)
    def _():
        _norm_modulate_rows(x_ref, g_ref, sh_ref, sc_ref, h_scr, tm)

    h = h_scr[...]
    a = jnp.dot(h, wg_ref[...].astype(BF16), preferred_element_type=F32)
    u = jnp.dot(h, wu_ref[...].astype(BF16), preferred_element_type=F32)
    o_ref[...] = (a * _sigmoid(a) * u).astype(o_ref.dtype)


def _row_specs(tm):
    return [pl.BlockSpec((tm, D), lambda i, j: (i, 0)),
            pl.BlockSpec((1, D), lambda i, j: (0, 0)),
            pl.BlockSpec((None, 1, D), lambda i, j: ((i * tm) // S, 0, 0)),
            pl.BlockSpec((None, 1, D), lambda i, j: ((i * tm) // S, 0, 0))]


def _in_proj_call(x, x_ctx, g, shift, scale, w, layer):
    tm, tn = 1024, _P_TN
    assert N_CTX == tm
    specs = _row_specs(tm)
    args = [x, g, shift, scale, w]
    if x_ctx is not None:
        last_lat = N_LAT // tm - 1
        specs[0] = pl.BlockSpec((tm, D), lambda i, j: (jnp.minimum(i, last_lat), 0))
        specs.insert(1, pl.BlockSpec((tm, D), lambda i, j: (0, 0), pipeline_mode=pl.Buffered(1)))
        args.insert(1, x_ctx)
    return pl.pallas_call(
        functools.partial(_in_proj_kernel, tm=tm, split_ctx=x_ctx is not None),
        out_shape=jax.ShapeDtypeStruct((N_TOK, _P_W), BF16),
        grid=(N_TOK // tm, _P_W // tn),
        in_specs=specs + [pl.BlockSpec((None, D, tn), lambda i, j: (layer, 0, j))],
        out_specs=pl.BlockSpec((tm, tn), lambda i, j: (i, j)),
        scratch_shapes=[pltpu.VMEM((tm, D), BF16)],
        compiler_params=_cparams(("parallel", "arbitrary")),
        name="in_proj",
    )(*args)


def _ffn_in_call(x, g, shift, scale, w, layer, *, rows):
    tm, tn = 1024, 512
    nj = D_FF // tn
    return pl.pallas_call(
        functools.partial(_ffn_in_kernel, tm=tm),
        out_shape=jax.ShapeDtypeStruct((rows, D_FF), BF16),
        grid=(rows // tm, nj),
        in_specs=_row_specs(tm) + [pl.BlockSpec((None, D, tn), lambda i, j: (layer, 0, j)),
                                   pl.BlockSpec((None, D, tn), lambda i, j: (layer, 0, nj + j))],
        out_specs=pl.BlockSpec((tm, tn), lambda i, j: (i, j)),
        scratch_shapes=[pltpu.VMEM((tm, D), BF16)],
        compiler_params=_cparams(("parallel", "arbitrary")),
        name="ffn_in",
    )(x, g, shift, scale, w, w)


def _resid_kernel(a_ref, w_ref, x_ref, g_ref, gate_ref, o_ref):
    z = jnp.dot(a_ref[...], w_ref[...], preferred_element_type=F32)
    ms = jnp.mean(z * z, axis=-1, keepdims=True)
    o_ref[...] = x_ref[...] + gate_ref[...] * (z * lax.rsqrt(ms + EPS) * g_ref[...])


def _resid_call(a, w, x, g, gate, layer, *, rows, tm, name):
    k = w.shape[1]
    return pl.pallas_call(
        _resid_kernel,
        out_shape=jax.ShapeDtypeStruct((rows, D), F32),
        grid=(rows // tm,),
        in_specs=[pl.BlockSpec((tm, k), lambda i: (i, 0)),
                  pl.BlockSpec((None, k, D), lambda i: (layer, 0, 0), pipeline_mode=pl.Buffered(1)),
                  pl.BlockSpec((tm, D), lambda i: (i, 0)),
                  pl.BlockSpec((1, D), lambda i: (0, 0)),
                  pl.BlockSpec((None, 1, D), lambda i: ((i * tm) // S, 0, 0))],
        out_specs=pl.BlockSpec((tm, D), lambda i: (i, 0)),
        compiler_params=_cparams(("parallel",)),
        name=name,
    )(a, w, x, g, gate)


def _head_norm_rope(xh, g, cos, sin, first_half):
    xf = xh.astype(F32)
    ms = jnp.mean(xf * xf, axis=-1, keepdims=True)
    y = xf * lax.rsqrt(ms + EPS) * g
    swapped = jnp.where(first_half, pltpu.roll(y, HD - HD // 4, 1), pltpu.roll(y, HD // 4, 1))
    return y * cos + swapped * sin


def _qkprep_kernel(q_ref, k_ref, v_ref, cos_ref, sin_ref, gq_ref, gk_ref, qt_ref, ko_ref, vt_ref,
                   *, lat_tiles):
    use_rope = pl.program_id(0) < lat_tiles
    cos = jnp.where(use_rope, cos_ref[...], 1.0)
    sin = jnp.where(use_rope, sin_ref[...], 0.0)
    lane = lax.broadcasted_iota(jnp.int32, (1, HD), 1)
    first_half = (lane % (HD // 2)) < (HD // 4)
    q_scale = HD ** -0.5 * math.log2(math.e)
    for h in range(ATT_H):
        cs = slice(h * HD, (h + 1) * HD)
        y = _head_norm_rope(q_ref[:, cs], gq_ref[...], cos, sin, first_half)
        qt_ref[cs, :] = (y * q_scale).T.astype(BF16)
    for h in range(ATT_KVH):
        cs = slice(h * HD, (h + 1) * HD)
        y = _head_norm_rope(k_ref[:, cs], gk_ref[...], cos, sin, first_half)
        ko_ref[:, cs] = y.astype(BF16)
        vt_ref[cs, :] = v_ref[:, cs].astype(F32).T.astype(BF16)


def _qkprep_call(p, cos_t, sin_t, gq, gk):
    tm = 512
    lat_tiles = N_LAT // tm
    pos_tiles = S // tm
    return pl.pallas_call(
        functools.partial(_qkprep_kernel, lat_tiles=lat_tiles),
        out_shape=(jax.ShapeDtypeStruct((ATT_QW, N_TOK), BF16),
                   jax.ShapeDtypeStruct((N_TOK, ATT_KVW), BF16),
                   jax.ShapeDtypeStruct((ATT_KVW, N_TOK), BF16)),
        grid=(N_TOK // tm,),
        in_specs=[pl.BlockSpec((tm, ATT_QW), lambda i: (i, _P_Q // ATT_QW)),
                  pl.BlockSpec((tm, ATT_KVW), lambda i: (i, _P_K // ATT_KVW)),
                  pl.BlockSpec((tm, ATT_KVW), lambda i: (i, _P_V // ATT_KVW)),
                  pl.BlockSpec((tm, HD), lambda i: (i % pos_tiles, 0)),
                  pl.BlockSpec((tm, HD), lambda i: (i % pos_tiles, 0)),
                  pl.BlockSpec((1, HD), lambda i: (0, 0)),
                  pl.BlockSpec((1, HD), lambda i: (0, 0))],
        out_specs=(pl.BlockSpec((ATT_QW, tm), lambda i: (0, i)),
                   pl.BlockSpec((tm, ATT_KVW), lambda i: (i, 0)),
                   pl.BlockSpec((ATT_KVW, tm), lambda i: (0, i))),
        compiler_params=_cparams(("parallel",)),
        name="qk_prep",
    )(p, p, p, cos_t, sin_t, gq, gk)


_TQ = 256
_KC = 512


def _attention_heads(qt_ref, key_refs, vt_refs, o_ref, s_scr, p_scr):
    chunks = []
    spans = []
    base = 0
    for k_ref in key_refs:
        n_keys = k_ref.shape[0]
        kc = min(_KC, n_keys)
        chunks += [(k_ref, c * kc, kc, base + c * kc) for c in range(n_keys // kc)]
        spans.append((base, n_keys))
        base += n_keys

    def scores(h):
        slot = h % 2
        qt = qt_ref[h * HD:(h + 1) * HD, :]
        mx = None
        for k_ref, r0, nr, s0 in chunks:
            sc = jnp.dot(k_ref[r0:r0 + nr, :], qt, preferred_element_type=F32)
            s_scr[slot, s0:s0 + nr, :] = sc
            part = jnp.max(sc.reshape(nr // 8, 8, _TQ), axis=0)
            mx = part if mx is None else jnp.maximum(mx, part)
        return jnp.max(mx, axis=0, keepdims=True)

    def probs(h, m):
        slot = h % 2
        tot = None
        for _, _, nr, s0 in chunks:
            pr = jnp.exp2(s_scr[slot, s0:s0 + nr, :] - m)
            p_scr[slot, s0:s0 + nr, :] = pr.astype(BF16)
            part = jnp.sum(pr.reshape(nr // 8, 8, _TQ), axis=0)
            tot = part if tot is None else tot + part
        return jnp.sum(tot, axis=0, keepdims=True)

    def output(h, denom):
        slot = h % 2
        ot = None
        for vt_ref, (s0, nr) in zip(vt_refs, spans):
            part = jnp.dot(vt_ref[...], p_scr[slot, s0:s0 + nr, :], preferred_element_type=F32)
            ot = part if ot is None else ot + part
        o_ref[:, h * HD:(h + 1) * HD] = (ot * (1.0 / denom)).T.astype(o_ref.dtype)

    m = scores(0)
    for h in range(ATT_G):
        m_next = scores(h + 1) if h + 1 < ATT_G else None
        output(h, probs(h, m))
        m = m_next


def _attn_latent_kernel(qt_ref, kl_ref, kc_ref, vtl_ref, vtc_ref, o_ref, s_scr, p_scr):
    _attention_heads(qt_ref, (kl_ref, kc_ref), (vtl_ref, vtc_ref), o_ref, s_scr, p_scr)


def _attn_ctx_kernel(qt_ref, kc_ref, vtc_ref, o_ref, s_scr, p_scr):
    _attention_heads(qt_ref, (kc_ref,), (vtc_ref,), o_ref, s_scr, p_scr)


def _attn_call(qt, k_proc, vt, *, with_ctx_queries):
    lat_tiles = S // _TQ
    ctx_blk = N_LAT // SC
    gw = ATT_G * HD

    att = pl.pallas_call(
        _attn_latent_kernel,
        out_shape=jax.ShapeDtypeStruct((N_LAT, ATT_QW), BF16),
        grid=(B, ATT_KVH, lat_tiles),
        in_specs=[pl.BlockSpec((gw, _TQ), lambda b, kv, t: (kv, b * lat_tiles + t)),
                  pl.BlockSpec((S, HD), lambda b, kv, t: (b, kv)),
                  pl.BlockSpec((SC, HD), lambda b, kv, t: (ctx_blk + b, kv)),
                  pl.BlockSpec((HD, S), lambda b, kv, t: (kv, b)),
                  pl.BlockSpec((HD, SC), lambda b, kv, t: (kv, ctx_blk + b))],
        out_specs=pl.BlockSpec((_TQ, gw), lambda b, kv, t: (b * lat_tiles + t, kv)),
        scratch_shapes=[pltpu.VMEM((2, S + SC, _TQ), F32),
                        pltpu.VMEM((2, S + SC, _TQ), BF16)],
        compiler_params=_cparams(("parallel", "parallel", "arbitrary")),
        name="gqa_attention",
    )(qt, k_proc, k_proc, vt, vt)
    if not with_ctx_queries:
        return att, None
    att_ctx = pl.pallas_call(
        _attn_ctx_kernel,
        out_shape=jax.ShapeDtypeStruct((N_CTX, ATT_QW), BF16),
        grid=(B, ATT_KVH),
        in_specs=[pl.BlockSpec((gw, SC), lambda b, kv: (kv, ctx_blk + b)),
                  pl.BlockSpec((SC, HD), lambda b, kv: (ctx_blk + b, kv)),
                  pl.BlockSpec((HD, SC), lambda b, kv: (kv, ctx_blk + b))],
        out_specs=pl.BlockSpec((SC, gw), lambda b, kv: (b, kv)),
        scratch_shapes=[pltpu.VMEM((2, SC, _TQ), F32),
                        pltpu.VMEM((2, SC, _TQ), BF16)],
        compiler_params=_cparams(("parallel", "parallel")),
        name="gqa_attention_ctx",
    )(qt, k_proc, vt)
    return att, att_ctx


_ML_CHUNKS = (SC + S) // ML_L
_ML_CTX_CHUNKS = SC // ML_L
_NEG = -1e30


def _log_sigmoid(x):
    return jnp.minimum(x, 0.0) - jnp.log1p(jnp.exp(-jnp.abs(x)))


def _exact_dot_f32(x, a01):
    hi = x.astype(BF16)
    r1 = x - hi.astype(F32)
    mid = r1.astype(BF16)
    lo = (r1 - mid.astype(F32)).astype(BF16)
    return (jnp.dot(hi, a01, preferred_element_type=F32)
            + jnp.dot(mid, a01, preferred_element_type=F32)
            + jnp.dot(lo, a01, preferred_element_type=F32))


_ML_EXT = ML_DV + LANES


_GS_FIELDS = 6
_GS_CHUNKS = 8


def _mlstm_gate_kernel(g_ref, bias_ref, o_ref):
    L = ML_L
    row = lax.broadcasted_iota(jnp.int32, (L, L), 0)
    col = lax.broadcasted_iota(jnp.int32, (L, L), 1)
    lane8 = lax.broadcasted_iota(jnp.int32, (8, L), 1)
    for c in range(_GS_CHUNKS):
        gates_t = (g_ref[c * L:(c + 1) * L, :].astype(F32) + bias_ref[...]).T
        for d in range(2):
            fwd = d == 0
            tri = jnp.where((row <= col) if fwd else (row >= col), 1.0, 0.0).astype(BF16)
            r8 = gates_t[d * 8:(d + 1) * 8, :]
            b = pltpu.roll(_exact_dot_f32(_log_sigmoid(r8), tri), 4, 0)
            u = r8 - b
            g = jnp.broadcast_to(b[:, L - 1:L] if fwd else b[:, 0:1], (8, L))
            w_end = g + u
            m_loc = jnp.broadcast_to(jnp.max(w_end, axis=1, keepdims=True), (8, L))
            e = jnp.exp(w_end - m_loc)
            run_max = u
            step = 1
            while step < L:
                if fwd:
                    shifted = jnp.where(lane8 >= step, pltpu.roll(run_max, step, 1), _NEG)
                else:
                    shifted = jnp.where(lane8 < L - step, pltpu.roll(run_max, L - step, 1), _NEG)
                run_max = jnp.maximum(run_max, shifted)
                step *= 2
            for i, field in enumerate((u, e, run_max, b, g, m_loc)):
                o_ref[c, d, i * 8:(i + 1) * 8, :] = field


def _mlstm_gate_call(p, gate_bias):
    rows = _GS_CHUNKS * ML_L
    return pl.pallas_call(
        _mlstm_gate_kernel,
        out_shape=jax.ShapeDtypeStruct((N_TOK // ML_L, 2, _GS_FIELDS * 8, ML_L), F32),
        grid=(N_TOK // rows,),
        in_specs=[pl.BlockSpec((rows, LANES), lambda i: (i, _P_IF // LANES)),
                  pl.BlockSpec((1, LANES), lambda i: (0, 0))],
        out_specs=pl.BlockSpec((_GS_CHUNKS, 2, _GS_FIELDS * 8, ML_L), lambda i: (i, 0, 0, 0)),
        compiler_params=_cparams(("parallel",)),
        name="mlstm_gates",
    )(p, gate_bias)


def _mlstm_kernel(qf_ref, kf_ref, vf_ref, gsf_ref, qb_ref, kb_ref, vb_ref, gsb_ref,
                  hf_ref, hb_ref, ct_scr, m_scr):
    @pl.when(pl.program_id(1) == 0)
    def _():
        ct_scr[...] = jnp.zeros_like(ct_scr)
        m_scr[...] = jnp.zeros_like(m_scr)

    L = ML_L
    row = lax.broadcasted_iota(jnp.int32, (L, L), 0)
    col = lax.broadcasted_iota(jnp.int32, (L, L), 1)
    ones_tile = jnp.where(lax.broadcasted_iota(jnp.int32, (L, LANES), 1) == 0, 1.0, 0.0).astype(BF16)
    q_scale = ML_DK ** -0.5

    directions = []
    for d, (q_ref, k_ref, v_ref, gs_ref, h_ref) in enumerate(
            ((qf_ref, kf_ref, vf_ref, gsf_ref, hf_ref), (qb_ref, kb_ref, vb_ref, gsb_ref, hb_ref))):
        mask = (col <= row) if d == 0 else (col >= row)
        u, e, run_max, b, g, m_loc = (gs_ref[i * 8:(i + 1) * 8, :] for i in range(_GS_FIELDS))
        m_prev = m_scr[d]
        m_new = jnp.maximum(g + m_prev, m_loc)
        a_old = jnp.exp(g + m_prev - m_new)
        a_loc = jnp.exp(m_loc - m_new)
        m_scr[d] = m_new
        big_m = jnp.maximum(m_prev, run_max)
        inter = jnp.exp(m_prev - big_m)
        floor = jnp.exp(-(b + big_m))
        cols = jnp.concatenate([big_m, inter, floor, jnp.zeros((L - 24, L), F32)], axis=0).T
        directions.append((d, q_ref, k_ref, v_ref, h_ref, mask, u, e, a_old, a_loc, cols))

    for h in range(ML_H):
        for d, q_ref, k_ref, v_ref, h_ref, mask, u, e, a_old, a_loc, cols in directions:
            idx = d * ML_H + h
            ct_prev = ct_scr[idx]
            q = q_ref[:, h * ML_DK:(h + 1) * ML_DK].astype(F32)
            qs = (q * q_scale).astype(BF16)
            qi = (q * (cols[:, 8 + h:9 + h] * q_scale)).astype(BF16)
            kt = k_ref[:, h * ML_DK:(h + 1) * ML_DK].astype(F32).T
            v_ext = jnp.concatenate([v_ref[:, h * ML_DV:(h + 1) * ML_DV], ones_tile], axis=1)

            qk = jnp.dot(qs, kt.astype(BF16), preferred_element_type=F32)
            expo = jnp.where(mask, u[h:h + 1, :] - cols[:, h:h + 1], _NEG)
            w = (jnp.exp(expo) * qk).astype(BF16)
            nd = jnp.dot(jnp.concatenate([w, qi], axis=1),
                         jnp.concatenate([v_ext, ct_prev.astype(BF16)], axis=0),
                         preferred_element_type=F32)
            den = nd[:, ML_DV:ML_DV + 1]
            inv = 1.0 / jnp.maximum(jnp.abs(den), cols[:, 16 + h:17 + h])
            h_ref[:, h * ML_DV:(h + 1) * ML_DV] = (nd[:, :ML_DV] * inv).astype(h_ref.dtype)

            c_loc = jnp.dot((kt * e[h:h + 1, :]).astype(BF16), v_ext, preferred_element_type=F32)
            keep = jnp.concatenate([a_old[h:h + 1, :]] * (_ML_EXT // LANES), axis=1)
            gain = jnp.concatenate([a_loc[h:h + 1, :]] * (_ML_EXT // LANES), axis=1)
            ct_scr[idx] = keep * ct_prev + gain * c_loc


def _mlstm_call(p, gate_stats):
    lat_chunks = S // ML_L
    ctx_blk = N_LAT // ML_L

    def fwd_rb(b, t):
        return jnp.where(t < _ML_CTX_CHUNKS, ctx_blk + b * _ML_CTX_CHUNKS + t,
                         b * lat_chunks + t - _ML_CTX_CHUNKS)

    def rev_rb(b, t):
        return jnp.where(t < _ML_CTX_CHUNKS, ctx_blk + b * _ML_CTX_CHUNKS + (_ML_CTX_CHUNKS - 1 - t),
                         b * lat_chunks + (_ML_CHUNKS - 1 - t))

    def specs(rb, d):
        return [pl.BlockSpec((ML_L, ML_QKW), lambda b, t: (rb(b, t), _P_MLQ // ML_QKW)),
                pl.BlockSpec((ML_L, ML_QKW), lambda b, t: (rb(b, t), _P_MLK // ML_QKW)),
                pl.BlockSpec((ML_L, ML_VW), lambda b, t: (rb(b, t), _P_MLV // ML_VW)),
                pl.BlockSpec((None, None, _GS_FIELDS * 8, ML_L), lambda b, t: (rb(b, t), d, 0, 0))]

    n_chain = 2 * ML_H
    return pl.pallas_call(
        _mlstm_kernel,
        out_shape=(jax.ShapeDtypeStruct((N_TOK, ML_VW), BF16),
                   jax.ShapeDtypeStruct((N_TOK, ML_VW), BF16)),
        grid=(B, _ML_CHUNKS),
        in_specs=specs(fwd_rb, 0) + specs(rev_rb, 1),
        out_specs=(pl.BlockSpec((ML_L, ML_VW), lambda b, t: (fwd_rb(b, t), 0)),
                   pl.BlockSpec((ML_L, ML_VW), lambda b, t: (rev_rb(b, t), 0))),
        scratch_shapes=[pltpu.VMEM((n_chain, ML_DK, _ML_EXT), F32),
                        pltpu.VMEM((2, 8, ML_L), F32)],
        compiler_params=_cparams(("parallel", "arbitrary")),
        name="mlstm",
    )(p, p, p, gate_stats, p, p, p, gate_stats)


def _pool_kernel(u_ref, w_ref, sc_ref, o_ref, mixed_scr, *, seq):
    gi = pl.program_id(1)
    t = lax.broadcasted_iota(jnp.int32, (seq, 1), 0)

    def prev(x, k):
        return jnp.where(t >= k, pltpu.roll(x, k, 0), 0.0)

    def nxt(x, k):
        return jnp.where(t < seq - k, pltpu.roll(x, seq - k, 0), 0.0)

    half_w = jnp.where(gi == 0, 1, jnp.where(gi == 1, 2, jnp.where(gi == 2, 4, 8)))
    count = (jnp.minimum(t + half_w, seq) - jnp.maximum(t - half_w, 0)).astype(F32)
    for c in range(POOL_G // LANES):
        cs = slice(c * LANES, (c + 1) * LANES)
        u = u_ref[:, cs].astype(F32)
        before = prev(u, 1)
        after = u
        win = before + after
        for step, k in enumerate((1, 2, 4)):
            before = before + prev(before, k)
            after = after + nxt(after, k)
            win = jnp.where(gi == step + 1, before + after, win)
        mixed_scr[:, cs] = (win / count - u).astype(BF16)
    o_ref[...] = (jnp.dot(mixed_scr[...], w_ref[...], preferred_element_type=F32)
                  * sc_ref[...]).astype(o_ref.dtype)


def _pool_call(p, pool_w, pool_scale, layer, *, seq, first_blk):
    n_groups = len(POOL_WINDOWS)
    return pl.pallas_call(
        functools.partial(_pool_kernel, seq=seq),
        out_shape=jax.ShapeDtypeStruct((B * seq, POOL_W), BF16),
        grid=(B, n_groups),
        in_specs=[pl.BlockSpec((seq, POOL_G), lambda b, g: (first_blk + b, _P_POOL // POOL_G + g)),
                  pl.BlockSpec((None, None, POOL_G, POOL_G), lambda b, g: (layer, g, 0, 0)),
                  pl.BlockSpec((None, 1, POOL_G), lambda b, g: (layer, 0, g))],
        out_specs=pl.BlockSpec((seq, POOL_G), lambda b, g: (b, g)),
        scratch_shapes=[pltpu.VMEM((seq, POOL_G), BF16)],
        compiler_params=_cparams(("parallel", "parallel")),
        name="pool_branch_%d" % seq,
    )(p, pool_w, pool_scale)


def _merge_kernel(*refs, split_ctx, lat_blocks):
    if split_ctx:
        (att_ref, attc_ref, hf_ref, hb_ref, og_ref, pool_ref, poolc_ref, g0_ref, g1_ref, g2_ref,
         wa_ref, wm_ref, wp_ref, wo_ref, mlg_ref, x_ref, xc_ref, ng_ref, gate_ref, o_ref, ml_scr) = refs
        is_ctx = pl.program_id(0) >= lat_blocks
        att = jnp.where(is_ctx, attc_ref[...], att_ref[...])
        pool = jnp.where(is_ctx, poolc_ref[...], pool_ref[...])
        x = jnp.where(is_ctx, xc_ref[...], x_ref[...])
    else:
        (att_ref, hf_ref, hb_ref, og_ref, pool_ref, g0_ref, g1_ref, g2_ref,
         wa_ref, wm_ref, wp_ref, wo_ref, mlg_ref, x_ref, ng_ref, gate_ref, o_ref, ml_scr) = refs
        att, pool, x = att_ref[...], pool_ref[...], x_ref[...]

    for h in range(ML_H):
        cs = slice(h * ML_DV, (h + 1) * ML_DV)
        hs = hf_ref[:, cs].astype(F32) + hb_ref[:, cs].astype(F32)
        ms = jnp.mean(hs * hs, axis=-1, keepdims=True)
        normed = hs * lax.rsqrt(ms + EPS) * mlg_ref[:, cs]
        ml_scr[:, cs] = (normed * og_ref[:, cs].astype(F32)).astype(BF16)

    def branch(g_ref, a, w_ref):
        return g_ref[...].astype(F32) * jnp.dot(a, w_ref[...], preferred_element_type=F32)

    y = branch(g0_ref, att, wa_ref) + branch(g1_ref, ml_scr[...], wm_ref) + branch(g2_ref, pool, wp_ref)
    z = jnp.dot(y.astype(BF16), wo_ref[...], preferred_element_type=F32)
    ms = jnp.mean(z * z, axis=-1, keepdims=True)
    o_ref[...] = x + gate_ref[...] * (z * lax.rsqrt(ms + EPS) * ng_ref[...])


def _merge_call(att, att_ctx, hf, hb, p, pool, pool_ctx, w_up_att, w_up_ml, w_up_pool, w_out, ml_head_g,
                x, x_ctx, ng, gate, layer):
    tm = 256
    split_ctx = att_ctx is not None
    rows = N_TOK if split_ctx else N_LAT
    lat_blocks = N_LAT // tm

    def row_blk(width, col=0):
        return pl.BlockSpec((tm, width), lambda i: (i, col))

    def lat_blk(width):
        if not split_ctx:
            return row_blk(width)
        return pl.BlockSpec((tm, width), lambda i: (jnp.minimum(i, lat_blocks - 1), 0))

    def ctx_blk(width):
        return pl.BlockSpec((tm, width), lambda i: (jnp.maximum(i - lat_blocks, 0), 0))

    def w_blk(k):
        return pl.BlockSpec((None, k, D), lambda i: (layer, 0, 0), pipeline_mode=pl.Buffered(1))

    def with_ctx(spec_arg, ctx_spec_arg):
        return [spec_arg, ctx_spec_arg] if split_ctx else [spec_arg]

    specs_args = (
        with_ctx((lat_blk(ATT_QW), att), (ctx_blk(ATT_QW), att_ctx))
        + [(row_blk(ML_VW), hf), (row_blk(ML_VW), hb), (row_blk(ML_VW, _P_O // ML_VW), p)]
        + with_ctx((lat_blk(POOL_W), pool), (ctx_blk(POOL_W), pool_ctx))
        + [(row_blk(D, _P_GATE // D + branch), p) for branch in range(3)]
        + [(w_blk(ATT_QW), w_up_att), (w_blk(ML_VW), w_up_ml), (w_blk(POOL_W), w_up_pool), (w_blk(D), w_out),
           (pl.BlockSpec((None, 1, ML_VW), lambda i: (layer, 0, 0)), ml_head_g)]
        + with_ctx((lat_blk(D), x), (ctx_blk(D), x_ctx))
        + [(pl.BlockSpec((1, D), lambda i: (0, 0)), ng),
           (pl.BlockSpec((None, 1, D), lambda i: ((i * tm) // S, 0, 0)), gate)])

    return pl.pallas_call(
        functools.partial(_merge_kernel, split_ctx=split_ctx, lat_blocks=lat_blocks),
        out_shape=jax.ShapeDtypeStruct((rows, D), F32),
        grid=(rows // tm,),
        in_specs=[sa[0] for sa in specs_args],
        out_specs=pl.BlockSpec((tm, D), lambda i: (i, 0)),
        scratch_shapes=[pltpu.VMEM((tm, ML_VW), BF16)],
        compiler_params=_cparams(("parallel",)),
        name="merge_out",
    )(*[sa[1] for sa in specs_args])


_N_IF = 4 * ML_H
_W_IN_COLS = 2 * ATT_KVW + ML_QKW + ML_VW + _N_IF + ATT_QW + ML_QKW + ML_VW + POOL_W + 3 * D
_W_IN_MAIN = (_W_IN_COLS // LANES) * LANES
_W_IN_IF = 2 * ATT_KVW + ML_QKW + ML_VW
_W_IN_SEGMENTS = (
    (0, ATT_KVW, _P_K),
    (ATT_KVW, ATT_KVW, _P_V),
    (2 * ATT_KVW, ML_QKW, _P_MLK),
    (2 * ATT_KVW + ML_QKW, ML_VW, _P_MLV),
    (_W_IN_IF + _N_IF, ATT_QW, _P_Q),
    (_W_IN_IF + _N_IF + ATT_QW, ML_QKW, _P_MLQ),
    (_W_IN_IF + _N_IF + ATT_QW + ML_QKW, ML_VW, _P_O),
    (_W_IN_IF + _N_IF + ATT_QW + ML_QKW + ML_VW, POOL_W, _P_POOL),
    (_W_IN_IF + _N_IF + ATT_QW + ML_QKW + ML_VW + POOL_W, 3 * D, _P_GATE),
)
_RG_TILE = 256


def _regroup_kernel(w_ref, tail_ref, o_ref):
    rows = w_ref.shape[0]
    lane = lax.broadcasted_iota(jnp.int32, (rows, LANES), 1)
    win_w = _RG_TILE + LANES
    src_i = lax.broadcasted_iota(jnp.int32, (win_w, _RG_TILE), 0)
    dst_i = lax.broadcasted_iota(jnp.int32, (win_w, _RG_TILE), 1)
    select = jnp.where(src_i == dst_i + _N_IF, 1.0, 0.0).astype(BF16)
    tail = jnp.where(lane < _W_IN_COLS - _W_IN_MAIN, tail_ref[...], 0.0).astype(BF16)

    for src, width, dst in _W_IN_SEGMENTS:
        if src % LANES == 0:
            o_ref[:, dst:dst + width] = w_ref[:, src:src + width].astype(BF16)
            continue
        assert src % LANES == _N_IF and width % _RG_TILE == 0
        for c in range(0, width, _RG_TILE):
            a0 = src + c - _N_IF
            if a0 + win_w <= _W_IN_MAIN:
                win = w_ref[:, a0:a0 + win_w].astype(BF16)
            else:
                assert a0 + _RG_TILE == _W_IN_MAIN
                win = jnp.concatenate([w_ref[:, a0:a0 + _RG_TILE].astype(BF16), tail], axis=1)
            o_ref[:, dst + c:dst + c + _RG_TILE] = jnp.dot(
                win, select, preferred_element_type=F32).astype(BF16)

    gate_cols = jnp.where(lane < _N_IF, w_ref[:, _W_IN_IF:_W_IN_IF + LANES], 0.0).astype(BF16)
    src_l = lax.broadcasted_iota(jnp.int32, (LANES, LANES), 0)
    dst_l = lax.broadcasted_iota(jnp.int32, (LANES, LANES), 1)
    target = ((src_l >> 2) & 1) * (2 * ML_H) + (src_l >> 3) * ML_H + (src_l & (ML_H - 1))
    reorder = jnp.where((src_l < _N_IF) & (dst_l == target), 1.0, 0.0).astype(BF16)
    o_ref[:, _P_IF:_P_IF + LANES] = jnp.dot(gate_cols, reorder, preferred_element_type=F32).astype(BF16)
    o_ref[:, _P_IF + LANES:] = jnp.zeros((rows, _P_W - _P_IF - LANES), BF16)


def _regroup_call(w_in):
    rows = 256
    return pl.pallas_call(
        _regroup_kernel,
        out_shape=jax.ShapeDtypeStruct((DEPTH, D, _P_W), BF16),
        grid=(DEPTH, D // rows),
        in_specs=[pl.BlockSpec((None, rows, _W_IN_MAIN), lambda l, r: (l, r, 0)),
                  pl.BlockSpec((None, rows, LANES), lambda l, r: (l, r, _W_IN_MAIN // LANES))],
        out_specs=pl.BlockSpec((None, rows, _P_W), lambda l, r: (l, r, 0)),
        compiler_params=_cparams(("parallel", "parallel")),
        name="w_in_regroup",
    )(w_in, w_in)


def _rope_tables():
    t = jnp.arange(S)
    row = (t // GRID_W).astype(F32)
    colp = (t % GRID_W).astype(F32)
    n_freq = HD // 4
    inv = ROPE_THETA ** (-jnp.arange(n_freq, dtype=F32) / n_freq)
    ar, ac = row[:, None] * inv, colp[:, None] * inv
    cos_t = jnp.concatenate([jnp.cos(ar), jnp.cos(ar), jnp.cos(ac), jnp.cos(ac)], axis=1)
    sin_t = jnp.concatenate([-jnp.sin(ar), jnp.sin(ar), -jnp.sin(ac), jnp.sin(ac)], axis=1)
    return cos_t, sin_t


def kernel(x, c, ctx, c_ctx, w_mod, b_mod, norm_g, w_in, ml_gate_b, qk_norm_g, ml_head_g,
           pool_w, pool_scale, w_up_att, w_up_ml, w_up_pool, w_out, w_ffn_in, w_ffn_out):
    cos_t, sin_t = _rope_tables()
    x_lat = x.reshape(N_LAT, D)
    x_ctx = ctx.reshape(N_CTX, D)
    c8 = jnp.concatenate([c, c_ctx[None], jnp.zeros((8 - B - 1, D), F32)], axis=0)

    w_in_r = _regroup_call(w_in)
    w_out_b = w_out.astype(BF16)
    w_ffn_out_b = w_ffn_out.astype(BF16)
    w_up_att_b, w_up_ml_b, w_up_pool_b = (w.astype(BF16) for w in (w_up_att, w_up_ml, w_up_pool))
    pool_w_b = pool_w.astype(BF16)
    pool_scale_r = pool_scale.reshape(DEPTH, 1, POOL_W)
    ml_head_g_r = ml_head_g.reshape(DEPTH, 1, ML_VW)
    gate_bias = jnp.pad(ml_gate_b.reshape(DEPTH, 1, _N_IF), ((0, 0), (0, 0), (0, LANES - _N_IF)))

    assert DEPTH == 2
    xa = None
    for l in range(DEPTH):
        last = l == DEPTH - 1
        rows = N_LAT if last else N_TOK

        mod = _mod_call(c8, w_mod, b_mod, l).reshape(8, 6, 1, D)
        sh_m, sc_m, gt_m, sh_f, sc_f, gt_f = (mod[:, i] for i in range(6))

        if l == 0:
            p = _in_proj_call(x_lat, x_ctx, norm_g[l, 0][None], sh_m, sc_m, w_in_r, l)
        else:
            p = _in_proj_call(xa, None, norm_g[l, 0][None], sh_m, sc_m, w_in_r, l)

        qt, k_proc, vt = _qkprep_call(p, cos_t, sin_t, qk_norm_g[l, 0][None], qk_norm_g[l, 1][None])
        att, att_ctx = _attn_call(qt, k_proc, vt, with_ctx_queries=not last)

        hf, hb = _mlstm_call(p, _mlstm_gate_call(p, gate_bias[l]))

        pool = _pool_call(p, pool_w_b, pool_scale_r, l, seq=S, first_blk=0)
        pool_ctx = None if last else _pool_call(p, pool_w_b, pool_scale_r, l, seq=SC, first_blk=N_LAT // SC)

        xa = _merge_call(att, att_ctx, hf, hb, p, pool, pool_ctx, w_up_att_b, w_up_ml_b, w_up_pool_b,
                         w_out_b, ml_head_g_r, x_lat if l == 0 else xa, x_ctx if l == 0 else None,
                         norm_g[l, 1][None], gt_m, l)

        f = _ffn_in_call(xa, norm_g[l, 2][None], sh_f, sc_f, w_ffn_in, l, rows=rows)
        xa = _resid_call(f, w_ffn_out_b, xa, norm_g[l, 3][None], gt_f, l, rows=rows, tm=256, name="ffn_out")

    return xa.reshape(B, S, D)
```

```python
import functools
import math

import jax
import jax.numpy as jnp
from jax import lax
from jax.experimental import pallas as pl
from jax.experimental.pallas import tpu as pltpu

F32 = jnp.float32
BF16 = jnp.bfloat16

D = 2048
B = 4
S = 4096
SC = 256
DEPTH = 2
GRID_W = 64
ATT_H = 8
ATT_KVH = 2
ATT_G = ATT_H // ATT_KVH
HD = 128
ROPE_THETA = 10000.0
ML_H = 4
ML_DK = 128
ML_DV = 256
ML_L = 128
POOL_WINDOWS = (2, 4, 8, 16)
POOL_G = 256
D_FF = 5632
EPS = 1e-6

N_LAT = B * S
N_CTX = B * SC
N_TOK = N_LAT + N_CTX

ATT_QW = ATT_H * HD
ATT_KVW = ATT_KVH * HD
ML_QKW = ML_H * ML_DK
ML_VW = ML_H * ML_DV
POOL_W = len(POOL_WINDOWS) * POOL_G

_P_GATE = 0
_P_O = 3 * D
_P_POOL = _P_O + ML_VW
_P_MLV = _P_POOL + POOL_W
_P_Q = _P_MLV + ML_VW
_P_MLK = _P_Q + ATT_QW
_P_MLQ = _P_MLK + ML_QKW
_P_K = _P_MLQ + ML_QKW
_P_V = _P_K + ATT_KVW
_P_IF = _P_V + ATT_KVW
_P_W = 12288
_P_TN = 1024
_P_SIGMOID_TILES = (_P_O + ML_VW) // _P_TN
LANES = 128

VMEM_LIMIT = 56 * 1024 * 1024


def _cparams(sem, vmem=VMEM_LIMIT):
    return pltpu.CompilerParams(dimension_semantics=sem, vmem_limit_bytes=vmem)


def _sigmoid(x):
    return 0.5 * jnp.tanh(0.5 * x) + 0.5


def _mod_kernel(c_ref, w_ref, b_ref, o_ref):
    c = c_ref[...]
    sc = (c * _sigmoid(c)).astype(BF16)
    o_ref[...] = jnp.dot(sc, w_ref[...].astype(BF16), preferred_element_type=F32) + b_ref[...]


def _mod_call(c8, w_mod, b_mod, layer):
    tn = 1024
    n = w_mod.shape[2]
    return pl.pallas_call(
        _mod_kernel,
        out_shape=jax.ShapeDtypeStruct((8, n), F32),
        grid=(n // tn,),
        in_specs=[pl.BlockSpec((8, D), lambda j: (0, 0)),
                  pl.BlockSpec((None, D, tn), lambda j: (layer, 0, j)),
                  pl.BlockSpec((None, 1, tn), lambda j: (layer, 0, j))],
        out_specs=pl.BlockSpec((8, tn), lambda j: (0, j)),
        compiler_params=_cparams(("parallel",)),
        name="mod_proj",
    )(c8, w_mod, b_mod.reshape(DEPTH, 1, n))


_ROW_CHUNK = 64


def _norm_modulate_rows(x_ref, g_ref, sh_ref, sc_ref, h_ref, tm):
    g = g_ref[...]
    one_plus = 1.0 + sc_ref[...]
    sh = sh_ref[...]

    def body(r, carry):
        rows = pl.ds(pl.multiple_of(r * _ROW_CHUNK, _ROW_CHUNK), _ROW_CHUNK)
        xf = x_ref[rows, :]
        ms = jnp.mean(xf * xf, axis=-1, keepdims=True)
        y = xf * lax.rsqrt(ms + EPS) * g
        h_ref[rows, :] = (y * one_plus + sh).astype(BF16)
        return carry

    lax.fori_loop(0, tm // _ROW_CHUNK, body, 0)


def _in_proj_kernel(*refs, tm, split_ctx):
    if split_ctx:
        x_ref, xc_ref, g_ref, sh_ref, sc_ref, w_ref, o_ref, h_scr = refs
    else:
        x_ref, g_ref, sh_ref, sc_ref, w_ref, o_ref, h_scr = refs
    i = pl.program_id(0)
    j = pl.program_id(1)
    first = j == 0
    if split_ctx:
        is_ctx = i >= N_LAT // tm

        @pl.when(first & jnp.logical_not(is_ctx))
        def _():
            _norm_modulate_rows(x_ref, g_ref, sh_ref, sc_ref, h_scr, tm)

        @pl.when(first & is_ctx)
        def _():
            _norm_modulate_rows(xc_ref, g_ref, sh_ref, sc_ref, h_scr, tm)
    else:
        @pl.when(first)
        def _():
            _norm_modulate_rows(x_ref, g_ref, sh_ref, sc_ref, h_scr, tm)

    acc = jnp.dot(h_scr[...], w_ref[...], preferred_element_type=F32)
    o_ref[...] = jnp.where(j < _P_SIGMOID_TILES, _sigmoid(acc), acc).astype(o_ref.dtype)


def _ffn_in_kernel(x_ref, g_ref, sh_ref, sc_ref, wg_ref, wu_ref, o_ref, h_scr, *, tm):
    @pl.when(pl.program_id(1) == 0)
    def _():
        _norm_modulate_rows(x_ref, g_ref, sh_ref, sc_ref, h_scr, tm)

    h = h_scr[...]
    a = jnp.dot(h, wg_ref[...].astype(BF16), preferred_element_type=F32)
    u = jnp.dot(h, wu_ref[...].astype(BF16), preferred_element_type=F32)
    o_ref[...] = (a * _sigmoid(a) * u).astype(o_ref.dtype)


def _row_specs(tm):
    return [pl.BlockSpec((tm, D), lambda i, j: (i, 0)),
            pl.BlockSpec((1, D), lambda i, j: (0, 0)),
            pl.BlockSpec((None, 1, D), lambda i, j: ((i * tm) // S, 0, 0)),
            pl.BlockSpec((None, 1, D), lambda i, j: ((i * tm) // S, 0, 0))]


def _in_proj_call(x, x_ctx, g, shift, scale, w, layer):
    tm, tn = 1024, _P_TN
    assert N_CTX == tm
    specs = _row_specs(tm)
    args = [x, g, shift, scale, w]
    if x_ctx is not None:
        last_lat = N_LAT // tm - 1
        specs[0] = pl.BlockSpec((tm, D), lambda i, j: (jnp.minimum(i, last_lat), 0))
        specs.insert(1, pl.BlockSpec((tm, D), lambda i, j: (0, 0), pipeline_mode=pl.Buffered(1)))
        args.insert(1, x_ctx)
    return pl.pallas_call(
        functools.partial(_in_proj_kernel, tm=tm, split_ctx=x_ctx is not None),
        out_shape=jax.ShapeDtypeStruct((N_TOK, _P_W), BF16),
        grid=(N_TOK // tm, _P_W // tn),
        in_specs=specs + [pl.BlockSpec((None, D, tn), lambda i, j: (layer, 0, j))],
        out_specs=pl.BlockSpec((tm, tn), lambda i, j: (i, j)),
        scratch_shapes=[pltpu.VMEM((tm, D), BF16)],
        compiler_params=_cparams(("parallel", "arbitrary")),
        name="in_proj",
    )(*args)


def _ffn_in_call(x, g, shift, scale, w, layer, *, rows):
    tm, tn = 1024, 512
    nj = D_FF // tn
    return pl.pallas_call(
        functools.partial(_ffn_in_kernel, tm=tm),
        out_shape=jax.ShapeDtypeStruct((rows, D_FF), BF16),
        grid=(rows // tm, nj),
        in_specs=_row_specs(tm) + [pl.BlockSpec((None, D, tn), lambda i, j: (layer, 0, j)),
                                   pl.BlockSpec((None, D, tn), lambda i, j: (layer, 0, nj + j))],
        out_specs=pl.BlockSpec((tm, tn), lambda i, j: (i, j)),
        scratch_shapes=[pltpu.VMEM((tm, D), BF16)],
        compiler_params=_cparams(("parallel", "arbitrary")),
        name="ffn_in",
    )(x, g, shift, scale, w, w)


def _resid_kernel(a_ref, w_ref, x_ref, g_ref, gate_ref, o_ref):
    z = jnp.dot(a_ref[...], w_ref[...], preferred_element_type=F32)
    ms = jnp.mean(z * z, axis=-1, keepdims=True)
    o_ref[...] = x_ref[...] + gate_ref[...] * (z * lax.rsqrt(ms + EPS) * g_ref[...])


def _resid_call(a, w, x, g, gate, layer, *, rows, tm, name):
    k = w.shape[1]
    return pl.pallas_call(
        _resid_kernel,
        out_shape=jax.ShapeDtypeStruct((rows, D), F32),
        grid=(rows // tm,),
        in_specs=[pl.BlockSpec((tm, k), lambda i: (i, 0)),
                  pl.BlockSpec((None, k, D), lambda i: (layer, 0, 0), pipeline_mode=pl.Buffered(1)),
                  pl.BlockSpec((tm, D), lambda i: (i, 0)),
                  pl.BlockSpec((1, D), lambda i: (0, 0)),
                  pl.BlockSpec((None, 1, D), lambda i: ((i * tm) // S, 0, 0))],
        out_specs=pl.BlockSpec((tm, D), lambda i: (i, 0)),
        compiler_params=_cparams(("parallel",)),
        name=name,
    )(a, w, x, g, gate)


def _head_norm_rope(xh, g, cos, sin, first_half):
    xf = xh.astype(F32)
    ms = jnp.mean(xf * xf, axis=-1, keepdims=True)
    y = xf * lax.rsqrt(ms + EPS) * g
    swapped = jnp.where(first_half, pltpu.roll(y, HD - HD // 4, 1), pltpu.roll(y, HD // 4, 1))
    return y * cos + swapped * sin


def _qkprep_kernel(q_ref, k_ref, v_ref, cos_ref, sin_ref, gq_ref, gk_ref, qt_ref, ko_ref, vt_ref,
                   *, lat_tiles):
    use_rope = pl.program_id(0) < lat_tiles
    cos = jnp.where(use_rope, cos_ref[...], 1.0)
    sin = jnp.where(use_rope, sin_ref[...], 0.0)
    lane = lax.broadcasted_iota(jnp.int32, (1, HD), 1)
    first_half = (lane % (HD // 2)) < (HD // 4)
    q_scale = HD ** -0.5 * math.log2(math.e)
    for h in range(ATT_H):
        cs = slice(h * HD, (h + 1) * HD)
        y = _head_norm_rope(q_ref[:, cs], gq_ref[...], cos, sin, first_half)
        qt_ref[cs, :] = (y * q_scale).T.astype(BF16)
    for h in range(ATT_KVH):
        cs = slice(h * HD, (h + 1) * HD)
        y = _head_norm_rope(k_ref[:, cs], gk_ref[...], cos, sin, first_half)
        ko_ref[:, cs] = y.astype(BF16)
        vt_ref[cs, :] = v_ref[:, cs].astype(F32).T.astype(BF16)


def _qkprep_call(p, cos_t, sin_t, gq, gk):
    tm = 512
    lat_tiles = N_LAT // tm
    pos_tiles = S // tm
    return pl.pallas_call(
        functools.partial(_qkprep_kernel, lat_tiles=lat_tiles),
        out_shape=(jax.ShapeDtypeStruct((ATT_QW, N_TOK), BF16),
                   jax.ShapeDtypeStruct((N_TOK, ATT_KVW), BF16),
                   jax.ShapeDtypeStruct((ATT_KVW, N_TOK), BF16)),
        grid=(N_TOK // tm,),
        in_specs=[pl.BlockSpec((tm, ATT_QW), lambda i: (i, _P_Q // ATT_QW)),
                  pl.BlockSpec((tm, ATT_KVW), lambda i: (i, _P_K // ATT_KVW)),
                  pl.BlockSpec((tm, ATT_KVW), lambda i: (i, _P_V // ATT_KVW)),
                  pl.BlockSpec((tm, HD), lambda i: (i % pos_tiles, 0)),
                  pl.BlockSpec((tm, HD), lambda i: (i % pos_tiles, 0)),
                  pl.BlockSpec((1, HD), lambda i: (0, 0)),
                  pl.BlockSpec((1, HD), lambda i: (0, 0))],
        out_specs=(pl.BlockSpec((ATT_QW, tm), lambda i: (0, i)),
                   pl.BlockSpec((tm, ATT_KVW), lambda i: (i, 0)),
                   pl.BlockSpec((ATT_KVW, tm), lambda i: (0, i))),
        compiler_params=_cparams(("parallel",)),
        name="qk_prep",
    )(p, p, p, cos_t, sin_t, gq, gk)


_TQ = 256
_KC = 1024


def _attention_heads(qt_ref, key_refs, vt_refs, o_ref, s_scr, p_scr, qt_next_ref=None, m_scr=None):
    chunks = []
    spans = []
    base = 0
    for k_ref in key_refs:
        n_keys = k_ref.shape[0]
        kc = min(_KC, n_keys)
        chunks += [(k_ref, c * kc, kc, base + c * kc) for c in range(n_keys // kc)]
        spans.append((base, n_keys))
        base += n_keys

    def scores(h, q_ref=qt_ref):
        slot = h % 2
        head = h % ATT_G
        qt = q_ref[head * HD:(head + 1) * HD, :]
        mx = None
        for k_ref, r0, nr, s0 in chunks:
            sc = jnp.dot(k_ref[r0:r0 + nr, :], qt, preferred_element_type=F32)
            s_scr[slot, s0:s0 + nr, :] = sc
            part = jnp.max(sc.reshape(nr // 8, 8, _TQ), axis=0)
            mx = part if mx is None else jnp.maximum(mx, part)
        return jnp.max(mx, axis=0, keepdims=True)

    def probs(h, m):
        slot = h % 2
        tot = None
        for _, _, nr, s0 in chunks:
            pr = jnp.exp2(s_scr[slot, s0:s0 + nr, :] - m)
            p_scr[slot, s0:s0 + nr, :] = pr.astype(BF16)
            part = jnp.sum(pr.reshape(nr // 8, 8, _TQ), axis=0)
            tot = part if tot is None else tot + part
        return jnp.sum(tot, axis=0, keepdims=True)

    def output(h, denom):
        slot = h % 2
        ot = None
        for vt_ref, (s0, nr) in zip(vt_refs, spans):
            part = jnp.dot(vt_ref[...], p_scr[slot, s0:s0 + nr, :], preferred_element_type=F32)
            ot = part if ot is None else ot + part
        o_ref[:, h * HD:(h + 1) * HD] = (ot * (1.0 / denom)).T.astype(o_ref.dtype)

    if qt_next_ref is None:
        m = scores(0)
        for h in range(ATT_G):
            m_next = scores(h + 1) if h + 1 < ATT_G else None
            output(h, probs(h, m))
            m = m_next
        return

    @pl.when(pl.program_id(2) == 0)
    def _():
        m_scr[...] = jnp.broadcast_to(scores(0), m_scr.shape)

    m = m_scr[0:1, :]
    for h in range(ATT_G):
        m_next = scores(h + 1) if h + 1 < ATT_G else scores(ATT_G, qt_next_ref)
        output(h, probs(h, m))
        m = m_next
    m_scr[...] = jnp.broadcast_to(m, m_scr.shape)


def _attn_latent_kernel(qt_ref, qt_next_ref, kl_ref, kc_ref, vtl_ref, vtc_ref, o_ref, s_scr, p_scr, m_scr):
    _attention_heads(qt_ref, (kl_ref, kc_ref), (vtl_ref, vtc_ref), o_ref, s_scr, p_scr, qt_next_ref, m_scr)


def _attn_ctx_kernel(qt_ref, kc_ref, vtc_ref, o_ref, s_scr, p_scr):
    _attention_heads(qt_ref, (kc_ref,), (vtc_ref,), o_ref, s_scr, p_scr)


def _attn_call(qt, k_proc, vt, *, with_ctx_queries):
    lat_tiles = S // _TQ
    ctx_blk = N_LAT // SC
    gw = ATT_G * HD

    att = pl.pallas_call(
        _attn_latent_kernel,
        out_shape=jax.ShapeDtypeStruct((N_LAT, ATT_QW), BF16),
        grid=(B, ATT_KVH, lat_tiles),
        in_specs=[pl.BlockSpec((gw, _TQ), lambda b, kv, t: (kv, b * lat_tiles + t)),
                  pl.BlockSpec((gw, _TQ),
                               lambda b, kv, t: (kv, b * lat_tiles + jnp.minimum(t + 1, lat_tiles - 1))),
                  pl.BlockSpec((S, HD), lambda b, kv, t: (b, kv)),
                  pl.BlockSpec((SC, HD), lambda b, kv, t: (ctx_blk + b, kv)),
                  pl.BlockSpec((HD, S), lambda b, kv, t: (kv, b)),
                  pl.BlockSpec((HD, SC), lambda b, kv, t: (kv, ctx_blk + b))],
        out_specs=pl.BlockSpec((_TQ, gw), lambda b, kv, t: (b * lat_tiles + t, kv)),
        scratch_shapes=[pltpu.VMEM((2, S + SC, _TQ), F32),
                        pltpu.VMEM((2, S + SC, _TQ), BF16),
                        pltpu.VMEM((8, _TQ), F32)],
        compiler_params=_cparams(("parallel", "parallel", "arbitrary")),
        name="gqa_attention",
    )(qt, qt, k_proc, k_proc, vt, vt)
    if not with_ctx_queries:
        return att, None
    att_ctx = pl.pallas_call(
        _attn_ctx_kernel,
        out_shape=jax.ShapeDtypeStruct((N_CTX, ATT_QW), BF16),
        grid=(B, ATT_KVH),
        in_specs=[pl.BlockSpec((gw, SC), lambda b, kv: (kv, ctx_blk + b)),
                  pl.BlockSpec((SC, HD), lambda b, kv: (ctx_blk + b, kv)),
                  pl.BlockSpec((HD, SC), lambda b, kv: (kv, ctx_blk + b))],
        out_specs=pl.BlockSpec((SC, gw), lambda b, kv: (b, kv)),
        scratch_shapes=[pltpu.VMEM((2, SC, _TQ), F32),
                        pltpu.VMEM((2, SC, _TQ), BF16)],
        compiler_params=_cparams(("parallel", "parallel")),
        name="gqa_attention_ctx",
    )(qt, k_proc, vt)
    return att, att_ctx


_ML_CHUNKS = (SC + S) // ML_L
_ML_CTX_CHUNKS = SC // ML_L
_NEG = -1e30


def _log_sigmoid(x):
    return jnp.minimum(x, 0.0) - jnp.log1p(jnp.exp(-jnp.abs(x)))


def _exact_dot_f32(x, a01):
    hi = x.astype(BF16)
    r1 = x - hi.astype(F32)
    mid = r1.astype(BF16)
    lo = (r1 - mid.astype(F32)).astype(BF16)
    return (jnp.dot(hi, a01, preferred_element_type=F32)
            + jnp.dot(mid, a01, preferred_element_type=F32)
            + jnp.dot(lo, a01, preferred_element_type=F32))


_ML_EXT = ML_DV + LANES


_GS_FIELDS = 6
_GS_CHUNKS = 8


def _mlstm_gate_kernel(g_ref, bias_ref, o_ref):
    L = ML_L
    row = lax.broadcasted_iota(jnp.int32, (L, L), 0)
    col = lax.broadcasted_iota(jnp.int32, (L, L), 1)
    lane8 = lax.broadcasted_iota(jnp.int32, (8, L), 1)
    for c in range(_GS_CHUNKS):
        gates_t = (g_ref[c * L:(c + 1) * L, :].astype(F32) + bias_ref[...]).T
        for d in range(2):
            fwd = d == 0
            tri = jnp.where((row <= col) if fwd else (row >= col), 1.0, 0.0).astype(BF16)
            r8 = gates_t[d * 8:(d + 1) * 8, :]
            b = pltpu.roll(_exact_dot_f32(_log_sigmoid(r8), tri), 4, 0)
            u = r8 - b
            g = jnp.broadcast_to(b[:, L - 1:L] if fwd else b[:, 0:1], (8, L))
            w_end = g + u
            m_loc = jnp.broadcast_to(jnp.max(w_end, axis=1, keepdims=True), (8, L))
            e = jnp.exp(w_end - m_loc)
            run_max = u
            step = 1
            while step < L:
                if fwd:
                    shifted = jnp.where(lane8 >= step, pltpu.roll(run_max, step, 1), _NEG)
                else:
                    shifted = jnp.where(lane8 < L - step, pltpu.roll(run_max, L - step, 1), _NEG)
                run_max = jnp.maximum(run_max, shifted)
                step *= 2
            for i, field in enumerate((u, e, run_max, b, g, m_loc)):
                o_ref[c, d, i * 8:(i + 1) * 8, :] = field


def _mlstm_gate_call(p, gate_bias):
    rows = _GS_CHUNKS * ML_L
    return pl.pallas_call(
        _mlstm_gate_kernel,
        out_shape=jax.ShapeDtypeStruct((N_TOK // ML_L, 2, _GS_FIELDS * 8, ML_L), F32),
        grid=(N_TOK // rows,),
        in_specs=[pl.BlockSpec((rows, LANES), lambda i: (i, _P_IF // LANES)),
                  pl.BlockSpec((1, LANES), lambda i: (0, 0))],
        out_specs=pl.BlockSpec((_GS_CHUNKS, 2, _GS_FIELDS * 8, ML_L), lambda i: (i, 0, 0, 0)),
        compiler_params=_cparams(("parallel",)),
        name="mlstm_gates",
    )(p, gate_bias)


def _mlstm_kernel(qf_ref, kf_ref, vf_ref, gsf_ref, qb_ref, kb_ref, vb_ref, gsb_ref,
                  hf_ref, hb_ref, ct_scr, m_scr):
    @pl.when(pl.program_id(1) == 0)
    def _():
        ct_scr[...] = jnp.zeros_like(ct_scr)
        m_scr[...] = jnp.zeros_like(m_scr)

    L = ML_L
    row = lax.broadcasted_iota(jnp.int32, (L, L), 0)
    col = lax.broadcasted_iota(jnp.int32, (L, L), 1)
    ones_tile = jnp.where(lax.broadcasted_iota(jnp.int32, (L, LANES), 1) == 0, 1.0, 0.0).astype(BF16)
    q_scale = ML_DK ** -0.5

    directions = []
    for d, (q_ref, k_ref, v_ref, gs_ref, h_ref) in enumerate(
            ((qf_ref, kf_ref, vf_ref, gsf_ref, hf_ref), (qb_ref, kb_ref, vb_ref, gsb_ref, hb_ref))):
        mask = (col <= row) if d == 0 else (col >= row)
        u, e, run_max, b, g, m_loc = (gs_ref[i * 8:(i + 1) * 8, :] for i in range(_GS_FIELDS))
        m_prev = m_scr[d]
        m_new = jnp.maximum(g + m_prev, m_loc)
        a_old = jnp.exp(g + m_prev - m_new)
        a_loc = jnp.exp(m_loc - m_new)
        m_scr[d] = m_new
        big_m = jnp.maximum(m_prev, run_max)
        inter = jnp.exp(m_prev - big_m)
        floor = jnp.exp(-(b + big_m))
        cols = jnp.concatenate([big_m, inter, floor, jnp.zeros((L - 24, L), F32)], axis=0).T
        directions.append((d, q_ref, k_ref, v_ref, h_ref, mask, u, e, a_old, a_loc, cols))

    for h in range(ML_H):
        for d, q_ref, k_ref, v_ref, h_ref, mask, u, e, a_old, a_loc, cols in directions:
            idx = d * ML_H + h
            ct_prev = ct_scr[idx]
            q = q_ref[:, h * ML_DK:(h + 1) * ML_DK].astype(F32)
            qs = (q * q_scale).astype(BF16)
            qi = (q * (cols[:, 8 + h:9 + h] * q_scale)).astype(BF16)
            kt = k_ref[:, h * ML_DK:(h + 1) * ML_DK].astype(F32).T
            v_ext = jnp.concatenate([v_ref[:, h * ML_DV:(h + 1) * ML_DV], ones_tile], axis=1)

            qk = jnp.dot(qs, kt.astype(BF16), preferred_element_type=F32)
            expo = jnp.where(mask, u[h:h + 1, :] - cols[:, h:h + 1], _NEG)
            w = (jnp.exp(expo) * qk).astype(BF16)
            nd = jnp.dot(jnp.concatenate([w, qi], axis=1),
                         jnp.concatenate([v_ext, ct_prev.astype(BF16)], axis=0),
                         preferred_element_type=F32)
            den = nd[:, ML_DV:ML_DV + 1]
            inv = 1.0 / jnp.maximum(jnp.abs(den), cols[:, 16 + h:17 + h])
            h_ref[:, h * ML_DV:(h + 1) * ML_DV] = (nd[:, :ML_DV] * inv).astype(h_ref.dtype)

            c_loc = jnp.dot((kt * e[h:h + 1, :]).astype(BF16), v_ext, preferred_element_type=F32)
            keep = jnp.concatenate([a_old[h:h + 1, :]] * (_ML_EXT // LANES), axis=1)
            gain = jnp.concatenate([a_loc[h:h + 1, :]] * (_ML_EXT // LANES), axis=1)
            ct_scr[idx] = keep * ct_prev + gain * c_loc


def _mlstm_call(p, gate_stats):
    lat_chunks = S // ML_L
    ctx_blk = N_LAT // ML_L

    def fwd_rb(b, t):
        return jnp.where(t < _ML_CTX_CHUNKS, ctx_blk + b * _ML_CTX_CHUNKS + t,
                         b * lat_chunks + t - _ML_CTX_CHUNKS)

    def rev_rb(b, t):
        return jnp.where(t < _ML_CTX_CHUNKS, ctx_blk + b * _ML_CTX_CHUNKS + (_ML_CTX_CHUNKS - 1 - t),
                         b * lat_chunks + (_ML_CHUNKS - 1 - t))

    def specs(rb, d):
        return [pl.BlockSpec((ML_L, ML_QKW), lambda b, t: (rb(b, t), _P_MLQ // ML_QKW)),
                pl.BlockSpec((ML_L, ML_QKW), lambda b, t: (rb(b, t), _P_MLK // ML_QKW)),
                pl.BlockSpec((ML_L, ML_VW), lambda b, t: (rb(b, t), _P_MLV // ML_VW)),
                pl.BlockSpec((None, None, _GS_FIELDS * 8, ML_L), lambda b, t: (rb(b, t), d, 0, 0))]

    n_chain = 2 * ML_H
    return pl.pallas_call(
        _mlstm_kernel,
        out_shape=(jax.ShapeDtypeStruct((N_TOK, ML_VW), BF16),
                   jax.ShapeDtypeStruct((N_TOK, ML_VW), BF16)),
        grid=(B, _ML_CHUNKS),
        in_specs=specs(fwd_rb, 0) + specs(rev_rb, 1),
        out_specs=(pl.BlockSpec((ML_L, ML_VW), lambda b, t: (fwd_rb(b, t), 0)),
                   pl.BlockSpec((ML_L, ML_VW), lambda b, t: (rev_rb(b, t), 0))),
        scratch_shapes=[pltpu.VMEM((n_chain, ML_DK, _ML_EXT), F32),
                        pltpu.VMEM((2, 8, ML_L), F32)],
        compiler_params=_cparams(("parallel", "arbitrary")),
        name="mlstm",
    )(p, p, p, gate_stats, p, p, p, gate_stats)


def _pool_kernel(u_ref, w_ref, sc_ref, o_ref, mixed_scr, *, seq):
    gi = pl.program_id(1)
    t = lax.broadcasted_iota(jnp.int32, (seq, 1), 0)

    def prev(x, k):
        return jnp.where(t >= k, pltpu.roll(x, k, 0), 0.0)

    def nxt(x, k):
        return jnp.where(t < seq - k, pltpu.roll(x, seq - k, 0), 0.0)

    def centred_residual(doublings):
        def fill():
            half_w = 2 ** doublings
            inv_count = 1.0 / (jnp.minimum(t + half_w, seq) - jnp.maximum(t - half_w, 0)).astype(F32)
            for c in range(POOL_G // LANES):
                cs = slice(c * LANES, (c + 1) * LANES)
                u = u_ref[:, cs].astype(F32)
                before = prev(u, 1)
                after = u
                for k in (1, 2, 4)[:doublings]:
                    before = before + prev(before, k)
                    after = after + nxt(after, k)
                mixed_scr[:, cs] = ((before + after) * inv_count - u).astype(BF16)
        return fill

    for group, window in enumerate(POOL_WINDOWS):
        pl.when(gi == group)(centred_residual(window.bit_length() - 2))
    o_ref[...] = (jnp.dot(mixed_scr[...], w_ref[...], preferred_element_type=F32)
                  * sc_ref[...]).astype(o_ref.dtype)


def _pool_call(p, pool_w, pool_scale, layer, *, seq, first_blk):
    n_groups = len(POOL_WINDOWS)
    return pl.pallas_call(
        functools.partial(_pool_kernel, seq=seq),
        out_shape=jax.ShapeDtypeStruct((B * seq, POOL_W), BF16),
        grid=(B, n_groups),
        in_specs=[pl.BlockSpec((seq, POOL_G), lambda b, g: (first_blk + b, _P_POOL // POOL_G + g)),
                  pl.BlockSpec((None, None, POOL_G, POOL_G), lambda b, g: (layer, g, 0, 0)),
                  pl.BlockSpec((None, 1, POOL_G), lambda b, g: (layer, 0, g))],
        out_specs=pl.BlockSpec((seq, POOL_G), lambda b, g: (b, g)),
        scratch_shapes=[pltpu.VMEM((seq, POOL_G), BF16)],
        compiler_params=_cparams(("parallel", "parallel")),
        name="pool_branch_%d" % seq,
    )(p, pool_w, pool_scale)


def _merge_kernel(*refs, split_ctx, lat_blocks):
    if split_ctx:
        (att_ref, attc_ref, hf_ref, hb_ref, og_ref, pool_ref, poolc_ref, g0_ref, g1_ref, g2_ref,
         wa_ref, wm_ref, wp_ref, wo_ref, mlg_ref, x_ref, xc_ref, ng_ref, gate_ref, o_ref, ml_scr) = refs
        is_ctx = pl.program_id(0) >= lat_blocks
        att = jnp.where(is_ctx, attc_ref[...], att_ref[...])
        pool = jnp.where(is_ctx, poolc_ref[...], pool_ref[...])
        x = jnp.where(is_ctx, xc_ref[...], x_ref[...])
    else:
        (att_ref, hf_ref, hb_ref, og_ref, pool_ref, g0_ref, g1_ref, g2_ref,
         wa_ref, wm_ref, wp_ref, wo_ref, mlg_ref, x_ref, ng_ref, gate_ref, o_ref, ml_scr) = refs
        att, pool, x = att_ref[...], pool_ref[...], x_ref[...]

    for h in range(ML_H):
        cs = slice(h * ML_DV, (h + 1) * ML_DV)
        hs = hf_ref[:, cs].astype(F32) + hb_ref[:, cs].astype(F32)
        ms = jnp.mean(hs * hs, axis=-1, keepdims=True)
        normed = hs * lax.rsqrt(ms + EPS) * mlg_ref[:, cs]
        ml_scr[:, cs] = (normed * og_ref[:, cs].astype(F32)).astype(BF16)

    def branch(g_ref, a, w_ref):
        return g_ref[...].astype(F32) * jnp.dot(a, w_ref[...], preferred_element_type=F32)

    y = branch(g0_ref, att, wa_ref) + branch(g1_ref, ml_scr[...], wm_ref) + branch(g2_ref, pool, wp_ref)
    z = jnp.dot(y.astype(BF16), wo_ref[...], preferred_element_type=F32)
    ms = jnp.mean(z * z, axis=-1, keepdims=True)
    o_ref[...] = x + gate_ref[...] * (z * lax.rsqrt(ms + EPS) * ng_ref[...])


def _merge_call(att, att_ctx, hf, hb, p, pool, pool_ctx, w_up_att, w_up_ml, w_up_pool, w_out, ml_head_g,
                x, x_ctx, ng, gate, layer):
    tm = 256
    split_ctx = att_ctx is not None
    rows = N_TOK if split_ctx else N_LAT
    lat_blocks = N_LAT // tm

    def row_blk(width, col=0):
        return pl.BlockSpec((tm, width), lambda i: (i, col))

    def lat_blk(width):
        if not split_ctx:
            return row_blk(width)
        return pl.BlockSpec((tm, width), lambda i: (jnp.minimum(i, lat_blocks - 1), 0))

    def ctx_blk(width):
        return pl.BlockSpec((tm, width), lambda i: (jnp.maximum(i - lat_blocks, 0), 0))

    def w_blk(k):
        return pl.BlockSpec((None, k, D), lambda i: (layer, 0, 0), pipeline_mode=pl.Buffered(1))

    def with_ctx(spec_arg, ctx_spec_arg):
        return [spec_arg, ctx_spec_arg] if split_ctx else [spec_arg]

    specs_args = (
        with_ctx((lat_blk(ATT_QW), att), (ctx_blk(ATT_QW), att_ctx))
        + [(row_blk(ML_VW), hf), (row_blk(ML_VW), hb), (row_blk(ML_VW, _P_O // ML_VW), p)]
        + with_ctx((lat_blk(POOL_W), pool), (ctx_blk(POOL_W), pool_ctx))
        + [(row_blk(D, _P_GATE // D + branch), p) for branch in range(3)]
        + [(w_blk(ATT_QW), w_up_att), (w_blk(ML_VW), w_up_ml), (w_blk(POOL_W), w_up_pool), (w_blk(D), w_out),
           (pl.BlockSpec((None, 1, ML_VW), lambda i: (layer, 0, 0)), ml_head_g)]
        + with_ctx((lat_blk(D), x), (ctx_blk(D), x_ctx))
        + [(pl.BlockSpec((1, D), lambda i: (0, 0)), ng),
           (pl.BlockSpec((None, 1, D), lambda i: ((i * tm) // S, 0, 0)), gate)])

    return pl.pallas_call(
        functools.partial(_merge_kernel, split_ctx=split_ctx, lat_blocks=lat_blocks),
        out_shape=jax.ShapeDtypeStruct((rows, D), F32),
        grid=(rows // tm,),
        in_specs=[sa[0] for sa in specs_args],
        out_specs=pl.BlockSpec((tm, D), lambda i: (i, 0)),
        scratch_shapes=[pltpu.VMEM((tm, ML_VW), BF16)],
        compiler_params=_cparams(("parallel",)),
        name="merge_out",
    )(*[sa[1] for sa in specs_args])


_N_IF = 4 * ML_H
_W_IN_IF = 2 * ATT_KVW + ML_QKW + ML_VW
_W_IN_SEGMENTS = (
    (0, 2 * ATT_KVW, _P_K),
    (2 * ATT_KVW, ML_QKW, _P_MLK),
    (2 * ATT_KVW + ML_QKW, ML_VW, _P_MLV),
    (_W_IN_IF + _N_IF, ATT_QW, _P_Q),
    (_W_IN_IF + _N_IF + ATT_QW, ML_QKW, _P_MLQ),
    (_W_IN_IF + _N_IF + ATT_QW + ML_QKW, ML_VW, _P_O),
    (_W_IN_IF + _N_IF + ATT_QW + ML_QKW + ML_VW, POOL_W, _P_POOL),
    (_W_IN_IF + _N_IF + ATT_QW + ML_QKW + ML_VW + POOL_W, 3 * D, _P_GATE),
)
_RG_TILE = 512
_RG_TILES = tuple((src + c, dst + c) for src, width, dst in _W_IN_SEGMENTS for c in range(0, width, _RG_TILE))
assert all(width % _RG_TILE == 0 for _, width, _ in _W_IN_SEGMENTS) and _P_W - _P_IF == _RG_TILE


def _regroup_kernel(src_ref, dst_ref, wt_ref, o_ref):
    del src_ref, dst_ref
    t = pl.program_id(1)

    @pl.when(t < len(_RG_TILES))
    def _():
        o_ref[...] = wt_ref[0].T.astype(BF16)

    @pl.when(t == len(_RG_TILES))
    def _():
        lane = lax.broadcasted_iota(jnp.int32, (D, LANES), 1)
        gate_cols = jnp.where(lane < _N_IF, wt_ref[0, 0:LANES, :].T, 0.0).astype(BF16)
        src_l = lax.broadcasted_iota(jnp.int32, (LANES, LANES), 0)
        dst_l = lax.broadcasted_iota(jnp.int32, (LANES, LANES), 1)
        target = ((src_l >> 2) & 1) * (2 * ML_H) + (src_l >> 3) * ML_H + (src_l & (ML_H - 1))
        reorder = jnp.where((src_l < _N_IF) & (dst_l == target), 1.0, 0.0).astype(BF16)
        o_ref[:, 0:LANES] = jnp.dot(gate_cols, reorder, preferred_element_type=F32).astype(BF16)
        o_ref[:, LANES:] = jnp.zeros((D, _RG_TILE - LANES), BF16)


def _regroup_call(w_in):
    wt = jnp.swapaxes(w_in, 1, 2)
    assert all(s % _N_IF == 0 for s, _ in _RG_TILES) and _W_IN_IF % _N_IF == 0
    src = jnp.array([s // _N_IF for s, _ in _RG_TILES] + [_W_IN_IF // _N_IF], jnp.int32)
    dst = jnp.array([d // _RG_TILE for _, d in _RG_TILES] + [_P_IF // _RG_TILE], jnp.int32)
    grid_spec = pltpu.PrefetchScalarGridSpec(
        num_scalar_prefetch=2,
        grid=(DEPTH, len(_RG_TILES) + 1),
        in_specs=[pl.BlockSpec((pl.Element(1), pl.Element(_RG_TILE), pl.Element(D)),
                               lambda l, t, src, dst: (l, src[t] * _N_IF, 0))],
        out_specs=pl.BlockSpec((None, D, _RG_TILE), lambda l, t, src, dst: (l, 0, dst[t])))
    return pl.pallas_call(
        _regroup_kernel,
        out_shape=jax.ShapeDtypeStruct((DEPTH, D, _P_W), BF16),
        grid_spec=grid_spec,
        compiler_params=_cparams(("parallel", "arbitrary")),
        name="w_in_regroup",
    )(src, dst, wt)


def _rope_tables():
    t = jnp.arange(S)
    row = (t // GRID_W).astype(F32)
    colp = (t % GRID_W).astype(F32)
    n_freq = HD // 4
    inv = ROPE_THETA ** (-jnp.arange(n_freq, dtype=F32) / n_freq)
    ar, ac = row[:, None] * inv, colp[:, None] * inv
    cos_t = jnp.concatenate([jnp.cos(ar), jnp.cos(ar), jnp.cos(ac), jnp.cos(ac)], axis=1)
    sin_t = jnp.concatenate([-jnp.sin(ar), jnp.sin(ar), -jnp.sin(ac), jnp.sin(ac)], axis=1)
    return cos_t, sin_t


def kernel(x, c, ctx, c_ctx, w_mod, b_mod, norm_g, w_in, ml_gate_b, qk_norm_g, ml_head_g,
           pool_w, pool_scale, w_up_att, w_up_ml, w_up_pool, w_out, w_ffn_in, w_ffn_out):
    cos_t, sin_t = _rope_tables()
    x_lat = x.reshape(N_LAT, D)
    x_ctx = ctx.reshape(N_CTX, D)
    c8 = jnp.concatenate([c, c_ctx[None], jnp.zeros((8 - B - 1, D), F32)], axis=0)

    w_in_r = _regroup_call(w_in)
    w_out_b = w_out.astype(BF16)
    w_ffn_out_b = w_ffn_out.astype(BF16)
    w_up_att_b, w_up_ml_b, w_up_pool_b = (w.astype(BF16) for w in (w_up_att, w_up_ml, w_up_pool))
    pool_w_b = pool_w.astype(BF16)
    pool_scale_r = pool_scale.reshape(DEPTH, 1, POOL_W)
    ml_head_g_r = ml_head_g.reshape(DEPTH, 1, ML_VW)
    gate_bias = jnp.pad(ml_gate_b.reshape(DEPTH, 1, _N_IF), ((0, 0), (0, 0), (0, LANES - _N_IF)))

    assert DEPTH == 2
    xa = None
    for l in range(DEPTH):
        last = l == DEPTH - 1
        rows = N_LAT if last else N_TOK

        mod = _mod_call(c8, w_mod, b_mod, l).reshape(8, 6, 1, D)
        sh_m, sc_m, gt_m, sh_f, sc_f, gt_f = (mod[:, i] for i in range(6))

        if l == 0:
            p = _in_proj_call(x_lat, x_ctx, norm_g[l, 0][None], sh_m, sc_m, w_in_r, l)
        else:
            p = _in_proj_call(xa, None, norm_g[l, 0][None], sh_m, sc_m, w_in_r, l)

        qt, k_proc, vt = _qkprep_call(p, cos_t, sin_t, qk_norm_g[l, 0][None], qk_norm_g[l, 1][None])
        att, att_ctx = _attn_call(qt, k_proc, vt, with_ctx_queries=not last)

        hf, hb = _mlstm_call(p, _mlstm_gate_call(p, gate_bias[l]))

        pool = _pool_call(p, pool_w_b, pool_scale_r, l, seq=S, first_blk=0)
        pool_ctx = None if last else _pool_call(p, pool_w_b, pool_scale_r, l, seq=SC, first_blk=N_LAT // SC)

        xa = _merge_call(att, att_ctx, hf, hb, p, pool, pool_ctx, w_up_att_b, w_up_ml_b, w_up_pool_b,
                         w_out_b, ml_head_g_r, x_lat if l == 0 else xa, x_ctx if l == 0 else None,
                         norm_g[l, 1][None], gt_m, l)

        f = _ffn_in_call(xa, norm_g[l, 2][None], sh_f, sc_f, w_ffn_in, l, rows=rows)
        xa = _resid_call(f, w_ffn_out_b, xa, norm_g[l, 3][None], gt_f, l, rows=rows, tm=256, name="ffn_out")

    return xa.reshape(B, S, D)
```

```python
import functools
import math

import jax
import jax.numpy as jnp
from jax import lax
from jax.experimental import pallas as pl
from jax.experimental.pallas import tpu as pltpu

F32 = jnp.float32
BF16 = jnp.bfloat16

D = 2048
B = 4
S = 4096
SC = 256
DEPTH = 2
GRID_W = 64
ATT_H = 8
ATT_KVH = 2
ATT_G = ATT_H // ATT_KVH
HD = 128
ROPE_THETA = 10000.0
ML_H = 4
ML_DK = 128
ML_DV = 256
ML_L = 128
POOL_WINDOWS = (2, 4, 8, 16)
POOL_G = 256
D_FF = 5632
EPS = 1e-6

N_LAT = B * S
N_CTX = B * SC
N_TOK = N_LAT + N_CTX

ATT_QW = ATT_H * HD
ATT_KVW = ATT_KVH * HD
ML_QKW = ML_H * ML_DK
ML_VW = ML_H * ML_DV
POOL_W = len(POOL_WINDOWS) * POOL_G

_P_GATE = 0
_P_O = 3 * D
_P_POOL = _P_O + ML_VW
_P_MLV = _P_POOL + POOL_W
_P_Q = _P_MLV + ML_VW
_P_MLK = _P_Q + ATT_QW
_P_MLQ = _P_MLK + ML_QKW
_P_K = _P_MLQ + ML_QKW
_P_V = _P_K + ATT_KVW
_P_IF = _P_V + ATT_KVW
_P_W = 12288
_P_TN = 1024
_P_SIGMOID_TILES = (_P_O + ML_VW) // _P_TN
LANES = 128

VMEM_LIMIT = 56 * 1024 * 1024


def _cparams(sem, vmem=VMEM_LIMIT):
    return pltpu.CompilerParams(dimension_semantics=sem, vmem_limit_bytes=vmem)


def _sigmoid(x):
    return 0.5 * jnp.tanh(0.5 * x) + 0.5


def _mod_kernel(c_ref, w_ref, b_ref, o_ref):
    c = c_ref[...]
    sc = (c * _sigmoid(c)).astype(BF16)
    o_ref[...] = jnp.dot(sc, w_ref[...].astype(BF16), preferred_element_type=F32) + b_ref[...]


def _mod_call(c8, w_mod, b_mod, layer):
    tn = 1024
    n = w_mod.shape[2]
    return pl.pallas_call(
        _mod_kernel,
        out_shape=jax.ShapeDtypeStruct((8, n), F32),
        grid=(n // tn,),
        in_specs=[pl.BlockSpec((8, D), lambda j: (0, 0)),
                  pl.BlockSpec((None, D, tn), lambda j: (layer, 0, j)),
                  pl.BlockSpec((None, 1, tn), lambda j: (layer, 0, j))],
        out_specs=pl.BlockSpec((8, tn), lambda j: (0, j)),
        compiler_params=_cparams(("parallel",)),
        name="mod_proj",
    )(c8, w_mod, b_mod.reshape(DEPTH, 1, n))


_ROW_CHUNK = 32


def _norm_modulate_rows(x_ref, g_ref, sh_ref, sc_ref, h_ref, tm):
    gain = g_ref[...] * (1.0 + sc_ref[...])
    sh = sh_ref[...]

    def body(r, carry):
        rows = pl.ds(pl.multiple_of(r * _ROW_CHUNK, _ROW_CHUNK), _ROW_CHUNK)
        xf = x_ref[rows, :]
        ms = jnp.mean(xf * xf, axis=-1, keepdims=True)
        h_ref[rows, :] = (xf * lax.rsqrt(ms + EPS) * gain + sh).astype(BF16)
        return carry

    lax.fori_loop(0, tm // _ROW_CHUNK, body, 0, unroll=4)


def _in_proj_kernel(*refs, tm, split_ctx):
    if split_ctx:
        x_ref, xc_ref, g_ref, sh_ref, sc_ref, w_ref, o_ref, h_scr = refs
    else:
        x_ref, g_ref, sh_ref, sc_ref, w_ref, o_ref, h_scr = refs
    i = pl.program_id(0)
    j = pl.program_id(1)
    first = j == 0
    if split_ctx:
        is_ctx = i >= N_LAT // tm

        @pl.when(first & jnp.logical_not(is_ctx))
        def _():
            _norm_modulate_rows(x_ref, g_ref, sh_ref, sc_ref, h_scr, tm)

        @pl.when(first & is_ctx)
        def _():
            _norm_modulate_rows(xc_ref, g_ref, sh_ref, sc_ref, h_scr, tm)
    else:
        @pl.when(first)
        def _():
            _norm_modulate_rows(x_ref, g_ref, sh_ref, sc_ref, h_scr, tm)

    acc = jnp.dot(h_scr[...], w_ref[...], preferred_element_type=F32)
    o_ref[...] = jnp.where(j < _P_SIGMOID_TILES, _sigmoid(acc), acc).astype(o_ref.dtype)


def _ffn_in_kernel(x_ref, g_ref, sh_ref, sc_ref, wg_ref, wu_ref, o_ref, h_scr, *, tm):
    @pl.when(pl.program_id(1) == 0)
    def _():
        _norm_modulate_rows(x_ref, g_ref, sh_ref, sc_ref, h_scr, tm)

    h = h_scr[...]
    a = jnp.dot(h, wg_ref[...].astype(BF16), preferred_element_type=F32)
    u = jnp.dot(h, wu_ref[...].astype(BF16), preferred_element_type=F32)
    o_ref[...] = (a * _sigmoid(a) * u).astype(o_ref.dtype)


def _row_specs(tm):
    return [pl.BlockSpec((tm, D), lambda i, j: (i, 0)),
            pl.BlockSpec((1, D), lambda i, j: (0, 0)),
            pl.BlockSpec((None, 1, D), lambda i, j: ((i * tm) // S, 0, 0)),
            pl.BlockSpec((None, 1, D), lambda i, j: ((i * tm) // S, 0, 0))]


def _in_proj_call(x, x_ctx, g, shift, scale, w, layer):
    tm, tn = 1024, _P_TN
    assert N_CTX == tm
    specs = _row_specs(tm)
    args = [x, g, shift, scale, w]
    if x_ctx is not None:
        last_lat = N_LAT // tm - 1
        specs[0] = pl.BlockSpec((tm, D), lambda i, j: (jnp.minimum(i, last_lat), 0))
        specs.insert(1, pl.BlockSpec((tm, D), lambda i, j: (0, 0), pipeline_mode=pl.Buffered(1)))
        args.insert(1, x_ctx)
    return pl.pallas_call(
        functools.partial(_in_proj_kernel, tm=tm, split_ctx=x_ctx is not None),
        out_shape=jax.ShapeDtypeStruct((N_TOK, _P_W), BF16),
        grid=(N_TOK // tm, _P_W // tn),
        in_specs=specs + [pl.BlockSpec((None, D, tn), lambda i, j: (layer, 0, j))],
        out_specs=pl.BlockSpec((tm, tn), lambda i, j: (i, j)),
        scratch_shapes=[pltpu.VMEM((tm, D), BF16)],
        compiler_params=_cparams(("parallel", "arbitrary")),
        name="in_proj",
    )(*args)


def _ffn_in_call(x, g, shift, scale, w, layer, *, rows):
    tm, tn = 1024, 512
    nj = D_FF // tn
    return pl.pallas_call(
        functools.partial(_ffn_in_kernel, tm=tm),
        out_shape=jax.ShapeDtypeStruct((rows, D_FF), BF16),
        grid=(rows // tm, nj),
        in_specs=_row_specs(tm) + [pl.BlockSpec((None, D, tn), lambda i, j: (layer, 0, j)),
                                   pl.BlockSpec((None, D, tn), lambda i, j: (layer, 0, nj + j))],
        out_specs=pl.BlockSpec((tm, tn), lambda i, j: (i, j)),
        scratch_shapes=[pltpu.VMEM((tm, D), BF16)],
        compiler_params=_cparams(("parallel", "arbitrary")),
        name="ffn_in",
    )(x, g, shift, scale, w, w)


def _resid_kernel(a_ref, w_ref, x_ref, g_ref, gate_ref, o_ref):
    z = jnp.dot(a_ref[...], w_ref[...], preferred_element_type=F32)
    ms = jnp.mean(z * z, axis=-1, keepdims=True)
    o_ref[...] = x_ref[...] + gate_ref[...] * (z * lax.rsqrt(ms + EPS) * g_ref[...])


def _resid_call(a, w, x, g, gate, layer, *, rows, tm, name):
    k = w.shape[1]
    return pl.pallas_call(
        _resid_kernel,
        out_shape=jax.ShapeDtypeStruct((rows, D), F32),
        grid=(rows // tm,),
        in_specs=[pl.BlockSpec((tm, k), lambda i: (i, 0)),
                  pl.BlockSpec((None, k, D), lambda i: (layer, 0, 0), pipeline_mode=pl.Buffered(1)),
                  pl.BlockSpec((tm, D), lambda i: (i, 0)),
                  pl.BlockSpec((1, D), lambda i: (0, 0)),
                  pl.BlockSpec((None, 1, D), lambda i: ((i * tm) // S, 0, 0))],
        out_specs=pl.BlockSpec((tm, D), lambda i: (i, 0)),
        compiler_params=_cparams(("parallel",)),
        name=name,
    )(a, w, x, g, gate)


def _head_norm_rope(xh, g, cos, sin, first_half):
    xf = xh.astype(F32)
    ms = jnp.mean(xf * xf, axis=-1, keepdims=True)
    y = xf * lax.rsqrt(ms + EPS) * g
    swapped = jnp.where(first_half, pltpu.roll(y, HD - HD // 4, 1), pltpu.roll(y, HD // 4, 1))
    return y * cos + swapped * sin


def _qkprep_kernel(q_ref, k_ref, v_ref, cos_ref, sin_ref, gq_ref, gk_ref, qt_ref, ko_ref, vt_ref,
                   *, lat_tiles):
    use_rope = pl.program_id(0) < lat_tiles
    cos = jnp.where(use_rope, cos_ref[...], 1.0)
    sin = jnp.where(use_rope, sin_ref[...], 0.0)
    lane = lax.broadcasted_iota(jnp.int32, (1, HD), 1)
    first_half = (lane % (HD // 2)) < (HD // 4)
    q_scale = HD ** -0.5 * math.log2(math.e)
    for h in range(ATT_H):
        cs = slice(h * HD, (h + 1) * HD)
        y = _head_norm_rope(q_ref[:, cs], gq_ref[...], cos, sin, first_half)
        qt_ref[cs, :] = (y * q_scale).T.astype(BF16)
    for h in range(ATT_KVH):
        cs = slice(h * HD, (h + 1) * HD)
        y = _head_norm_rope(k_ref[:, cs], gk_ref[...], cos, sin, first_half)
        ko_ref[:, cs] = y.astype(BF16)
        vt_ref[cs, :] = v_ref[:, cs].astype(F32).T.astype(BF16)


def _qkprep_call(p, cos_t, sin_t, gq, gk):
    tm = 512
    lat_tiles = N_LAT // tm
    pos_tiles = S // tm
    return pl.pallas_call(
        functools.partial(_qkprep_kernel, lat_tiles=lat_tiles),
        out_shape=(jax.ShapeDtypeStruct((ATT_QW, N_TOK), BF16),
                   jax.ShapeDtypeStruct((N_TOK, ATT_KVW), BF16),
                   jax.ShapeDtypeStruct((ATT_KVW, N_TOK), BF16)),
        grid=(N_TOK // tm,),
        in_specs=[pl.BlockSpec((tm, ATT_QW), lambda i: (i, _P_Q // ATT_QW)),
                  pl.BlockSpec((tm, ATT_KVW), lambda i: (i, _P_K // ATT_KVW)),
                  pl.BlockSpec((tm, ATT_KVW), lambda i: (i, _P_V // ATT_KVW)),
                  pl.BlockSpec((tm, HD), lambda i: (i % pos_tiles, 0)),
                  pl.BlockSpec((tm, HD), lambda i: (i % pos_tiles, 0)),
                  pl.BlockSpec((1, HD), lambda i: (0, 0)),
                  pl.BlockSpec((1, HD), lambda i: (0, 0))],
        out_specs=(pl.BlockSpec((ATT_QW, tm), lambda i: (0, i)),
                   pl.BlockSpec((tm, ATT_KVW), lambda i: (i, 0)),
                   pl.BlockSpec((ATT_KVW, tm), lambda i: (0, i))),
        compiler_params=_cparams(("parallel",)),
        name="qk_prep",
    )(p, p, p, cos_t, sin_t, gq, gk)


_TQ = 256
_KC = 1024


def _attention_heads(qt_ref, key_refs, vt_refs, o_ref, s_scr, qt_next_ref=None, m_scr=None):
    tq = qt_ref.shape[1]
    chunks = []
    base = 0
    for k_ref, vt_ref in zip(key_refs, vt_refs):
        n_keys = k_ref.shape[0]
        kc = min(_KC, n_keys)
        chunks += [(k_ref, vt_ref, c * kc, kc, base + c * kc) for c in range(n_keys // kc)]
        base += n_keys

    def fold(x, op):
        return op(x.reshape(x.shape[0] // 8, 8, tq), axis=0)

    def score_chunk(q_ref, head, slot, chunk):
        k_ref, _, r0, nr, s0 = chunk
        sc = jnp.dot(k_ref[r0:r0 + nr, :], q_ref[head * HD:(head + 1) * HD, :],
                     preferred_element_type=F32)
        s_scr[slot, s0:s0 + nr, :] = sc
        return fold(sc, jnp.max)

    def scores(q_ref, head, slot):
        mx = functools.reduce(jnp.maximum, [score_chunk(q_ref, head, slot, ch) for ch in chunks])
        return jnp.max(mx, axis=0, keepdims=True)

    def head_step(h, m, nxt):
        slot = h % 2
        mx = tot = ot = None
        for ch in chunks:
            if nxt is not None:
                part = score_chunk(nxt[0], nxt[1], 1 - slot, ch)
                mx = part if mx is None else jnp.maximum(mx, part)
            _, vt_ref, r0, nr, s0 = ch
            pr = jnp.exp2(s_scr[slot, s0:s0 + nr, :] - m)
            psum = fold(pr, jnp.sum)
            tot = psum if tot is None else tot + psum
            pv = jnp.dot(vt_ref[:, r0:r0 + nr], pr.astype(BF16), preferred_element_type=F32)
            ot = pv if ot is None else ot + pv
        denom = jnp.sum(tot, axis=0, keepdims=True)
        o_ref[:, h * HD:(h + 1) * HD] = (ot * (1.0 / denom)).T.astype(o_ref.dtype)
        return None if mx is None else jnp.max(mx, axis=0, keepdims=True)

    if qt_next_ref is None:
        m = scores(qt_ref, 0, 0)
        for h in range(ATT_G):
            m = head_step(h, m, (qt_ref, h + 1) if h + 1 < ATT_G else None)
        return

    @pl.when(pl.program_id(2) == 0)
    def _():
        m_scr[...] = jnp.broadcast_to(scores(qt_ref, 0, 0), m_scr.shape)

    m = m_scr[0:1, :]
    for h in range(ATT_G):
        m = head_step(h, m, (qt_ref, h + 1) if h + 1 < ATT_G else (qt_next_ref, 0))
    m_scr[...] = jnp.broadcast_to(m, m_scr.shape)


def _attn_latent_kernel(qt_ref, qt_next_ref, kl_ref, kc_ref, vtl_ref, vtc_ref, o_ref, s_scr, m_scr):
    _attention_heads(qt_ref, (kl_ref, kc_ref), (vtl_ref, vtc_ref), o_ref, s_scr, qt_next_ref, m_scr)


def _attn_ctx_kernel(qt_ref, kc_ref, vtc_ref, o_ref, s_scr):
    _attention_heads(qt_ref, (kc_ref,), (vtc_ref,), o_ref, s_scr)


def _attn_call(qt, k_proc, vt, *, with_ctx_queries):
    lat_tiles = S // _TQ
    ctx_blk = N_LAT // SC
    gw = ATT_G * HD

    att = pl.pallas_call(
        _attn_latent_kernel,
        out_shape=jax.ShapeDtypeStruct((N_LAT, ATT_QW), BF16),
        grid=(B, ATT_KVH, lat_tiles),
        in_specs=[pl.BlockSpec((gw, _TQ), lambda b, kv, t: (kv, b * lat_tiles + t)),
                  pl.BlockSpec((gw, _TQ),
                               lambda b, kv, t: (kv, b * lat_tiles + jnp.minimum(t + 1, lat_tiles - 1))),
                  pl.BlockSpec((S, HD), lambda b, kv, t: (b, kv)),
                  pl.BlockSpec((SC, HD), lambda b, kv, t: (ctx_blk + b, kv)),
                  pl.BlockSpec((HD, S), lambda b, kv, t: (kv, b)),
                  pl.BlockSpec((HD, SC), lambda b, kv, t: (kv, ctx_blk + b))],
        out_specs=pl.BlockSpec((_TQ, gw), lambda b, kv, t: (b * lat_tiles + t, kv)),
        scratch_shapes=[pltpu.VMEM((2, S + SC, _TQ), F32),
                        pltpu.VMEM((8, _TQ), F32)],
        compiler_params=_cparams(("parallel", "parallel", "arbitrary")),
        name="gqa_attention",
    )(qt, qt, k_proc, k_proc, vt, vt)
    if not with_ctx_queries:
        return att, None
    att_ctx = pl.pallas_call(
        _attn_ctx_kernel,
        out_shape=jax.ShapeDtypeStruct((N_CTX, ATT_QW), BF16),
        grid=(B, ATT_KVH),
        in_specs=[pl.BlockSpec((gw, SC), lambda b, kv: (kv, ctx_blk + b)),
                  pl.BlockSpec((SC, HD), lambda b, kv: (ctx_blk + b, kv)),
                  pl.BlockSpec((HD, SC), lambda b, kv: (kv, ctx_blk + b))],
        out_specs=pl.BlockSpec((SC, gw), lambda b, kv: (b, kv)),
        scratch_shapes=[pltpu.VMEM((2, SC, SC), F32)],
        compiler_params=_cparams(("parallel", "parallel")),
        name="gqa_attention_ctx",
    )(qt, k_proc, vt)
    return att, att_ctx


_ML_CHUNKS = (SC + S) // ML_L
_ML_CTX_CHUNKS = SC // ML_L
_NEG = -1e30


def _log_sigmoid(x):
    return jnp.minimum(x, 0.0) - jnp.log1p(jnp.exp(-jnp.abs(x)))


def _exact_dot_f32(x, a01):
    hi = x.astype(BF16)
    r1 = x - hi.astype(F32)
    mid = r1.astype(BF16)
    lo = (r1 - mid.astype(F32)).astype(BF16)
    return (jnp.dot(hi, a01, preferred_element_type=F32)
            + jnp.dot(mid, a01, preferred_element_type=F32)
            + jnp.dot(lo, a01, preferred_element_type=F32))


_ML_EXT = ML_DV + LANES


_GS_FIELDS = 6
_GS_CHUNKS = 8


def _mlstm_gate_kernel(g_ref, bias_ref, o_ref):
    L = ML_L
    row = lax.broadcasted_iota(jnp.int32, (L, L), 0)
    col = lax.broadcasted_iota(jnp.int32, (L, L), 1)
    lane8 = lax.broadcasted_iota(jnp.int32, (8, L), 1)
    for c in range(_GS_CHUNKS):
        gates_t = (g_ref[c * L:(c + 1) * L, :].astype(F32) + bias_ref[...]).T
        for d in range(2):
            fwd = d == 0
            tri = jnp.where((row <= col) if fwd else (row >= col), 1.0, 0.0).astype(BF16)
            r8 = gates_t[d * 8:(d + 1) * 8, :]
            b = pltpu.roll(_exact_dot_f32(_log_sigmoid(r8), tri), 4, 0)
            u = r8 - b
            g = jnp.broadcast_to(b[:, L - 1:L] if fwd else b[:, 0:1], (8, L))
            w_end = g + u
            m_loc = jnp.broadcast_to(jnp.max(w_end, axis=1, keepdims=True), (8, L))
            e = jnp.exp(w_end - m_loc)
            run_max = u
            step = 1
            while step < L:
                if fwd:
                    shifted = jnp.where(lane8 >= step, pltpu.roll(run_max, step, 1), _NEG)
                else:
                    shifted = jnp.where(lane8 < L - step, pltpu.roll(run_max, L - step, 1), _NEG)
                run_max = jnp.maximum(run_max, shifted)
                step *= 2
            for i, field in enumerate((u, e, run_max, b, g, m_loc)):
                o_ref[c, d, i * 8:(i + 1) * 8, :] = field


def _mlstm_gate_call(p, gate_bias):
    rows = _GS_CHUNKS * ML_L
    return pl.pallas_call(
        _mlstm_gate_kernel,
        out_shape=jax.ShapeDtypeStruct((N_TOK // ML_L, 2, _GS_FIELDS * 8, ML_L), F32),
        grid=(N_TOK // rows,),
        in_specs=[pl.BlockSpec((rows, LANES), lambda i: (i, _P_IF // LANES)),
                  pl.BlockSpec((1, LANES), lambda i: (0, 0))],
        out_specs=pl.BlockSpec((_GS_CHUNKS, 2, _GS_FIELDS * 8, ML_L), lambda i: (i, 0, 0, 0)),
        compiler_params=_cparams(("parallel",)),
        name="mlstm_gates",
    )(p, gate_bias)


def _mlstm_kernel(qf_ref, kf_ref, vf_ref, gsf_ref, qb_ref, kb_ref, vb_ref, gsb_ref,
                  hf_ref, hb_ref, ct_scr, m_scr):
    @pl.when(pl.program_id(1) == 0)
    def _():
        ct_scr[...] = jnp.zeros_like(ct_scr)
        m_scr[...] = jnp.zeros_like(m_scr)

    L = ML_L
    row = lax.broadcasted_iota(jnp.int32, (L, L), 0)
    col = lax.broadcasted_iota(jnp.int32, (L, L), 1)
    ones_tile = jnp.where(lax.broadcasted_iota(jnp.int32, (L, LANES), 1) == 0, 1.0, 0.0).astype(BF16)
    q_scale = ML_DK ** -0.5

    directions = []
    for d, (q_ref, k_ref, v_ref, gs_ref, h_ref) in enumerate(
            ((qf_ref, kf_ref, vf_ref, gsf_ref, hf_ref), (qb_ref, kb_ref, vb_ref, gsb_ref, hb_ref))):
        mask = (col <= row) if d == 0 else (col >= row)
        u, e, run_max, b, g, m_loc = (gs_ref[i * 8:(i + 1) * 8, :] for i in range(_GS_FIELDS))
        m_prev = m_scr[d]
        m_new = jnp.maximum(g + m_prev, m_loc)
        a_old = jnp.exp(g + m_prev - m_new)
        a_loc = jnp.exp(m_loc - m_new)
        m_scr[d] = m_new
        big_m = jnp.maximum(m_prev, run_max)
        inter = jnp.exp(m_prev - big_m)
        floor = jnp.exp(-(b + big_m))
        cols = jnp.concatenate([big_m, inter, floor, jnp.zeros((L - 24, L), F32)], axis=0).T
        directions.append((d, q_ref, k_ref, v_ref, h_ref, mask, u, e, a_old, a_loc, cols))

    for h in range(ML_H):
        for d, q_ref, k_ref, v_ref, h_ref, mask, u, e, a_old, a_loc, cols in directions:
            idx = d * ML_H + h
            ct_prev = ct_scr[idx]
            q = q_ref[:, h * ML_DK:(h + 1) * ML_DK].astype(F32)
            qs = (q * q_scale).astype(BF16)
            qi = (q * (cols[:, 8 + h:9 + h] * q_scale)).astype(BF16)
            kt = k_ref[:, h * ML_DK:(h + 1) * ML_DK].astype(F32).T
            v_ext = jnp.concatenate([v_ref[:, h * ML_DV:(h + 1) * ML_DV], ones_tile], axis=1)

            qk = jnp.dot(qs, kt.astype(BF16), preferred_element_type=F32)
            expo = jnp.where(mask, u[h:h + 1, :] - cols[:, h:h + 1], _NEG)
            w = (jnp.exp(expo) * qk).astype(BF16)
            nd = jnp.dot(jnp.concatenate([w, qi], axis=1),
                         jnp.concatenate([v_ext, ct_prev.astype(BF16)], axis=0),
                         preferred_element_type=F32)
            den = nd[:, ML_DV:ML_DV + 1]
            inv = 1.0 / jnp.maximum(jnp.abs(den), cols[:, 16 + h:17 + h])
            h_ref[:, h * ML_DV:(h + 1) * ML_DV] = (nd[:, :ML_DV] * inv).astype(h_ref.dtype)

            c_loc = jnp.dot((kt * e[h:h + 1, :]).astype(BF16), v_ext, preferred_element_type=F32)
            keep = jnp.concatenate([a_old[h:h + 1, :]] * (_ML_EXT // LANES), axis=1)
            gain = jnp.concatenate([a_loc[h:h + 1, :]] * (_ML_EXT // LANES), axis=1)
            ct_scr[idx] = keep * ct_prev + gain * c_loc


def _mlstm_call(p, gate_stats):
    lat_chunks = S // ML_L
    ctx_blk = N_LAT // ML_L

    def fwd_rb(b, t):
        return jnp.where(t < _ML_CTX_CHUNKS, ctx_blk + b * _ML_CTX_CHUNKS + t,
                         b * lat_chunks + t - _ML_CTX_CHUNKS)

    def rev_rb(b, t):
        return jnp.where(t < _ML_CTX_CHUNKS, ctx_blk + b * _ML_CTX_CHUNKS + (_ML_CTX_CHUNKS - 1 - t),
                         b * lat_chunks + (_ML_CHUNKS - 1 - t))

    def specs(rb, d):
        return [pl.BlockSpec((ML_L, ML_QKW), lambda b, t: (rb(b, t), _P_MLQ // ML_QKW)),
                pl.BlockSpec((ML_L, ML_QKW), lambda b, t: (rb(b, t), _P_MLK // ML_QKW)),
                pl.BlockSpec((ML_L, ML_VW), lambda b, t: (rb(b, t), _P_MLV // ML_VW)),
                pl.BlockSpec((None, None, _GS_FIELDS * 8, ML_L), lambda b, t: (rb(b, t), d, 0, 0))]

    n_chain = 2 * ML_H
    return pl.pallas_call(
        _mlstm_kernel,
        out_shape=(jax.ShapeDtypeStruct((N_TOK, ML_VW), BF16),
                   jax.ShapeDtypeStruct((N_TOK, ML_VW), BF16)),
        grid=(B, _ML_CHUNKS),
        in_specs=specs(fwd_rb, 0) + specs(rev_rb, 1),
        out_specs=(pl.BlockSpec((ML_L, ML_VW), lambda b, t: (fwd_rb(b, t), 0)),
                   pl.BlockSpec((ML_L, ML_VW), lambda b, t: (rev_rb(b, t), 0))),
        scratch_shapes=[pltpu.VMEM((n_chain, ML_DK, _ML_EXT), F32),
                        pltpu.VMEM((2, 8, ML_L), F32)],
        compiler_params=_cparams(("parallel", "arbitrary")),
        name="mlstm",
    )(p, p, p, gate_stats, p, p, p, gate_stats)


def _pool_kernel(u_ref, w_ref, sc_ref, o_ref, mixed_scr, *, seq):
    gi = pl.program_id(1)
    t = lax.broadcasted_iota(jnp.int32, (seq, 1), 0)

    def prev(x, k):
        return jnp.where(t >= k, pltpu.roll(x, k, 0), 0.0)

    def nxt(x, k):
        return jnp.where(t < seq - k, pltpu.roll(x, seq - k, 0), 0.0)

    def centred_residual(doublings):
        def fill():
            half_w = 2 ** doublings
            inv_count = 1.0 / (jnp.minimum(t + half_w, seq) - jnp.maximum(t - half_w, 0)).astype(F32)
            for c in range(POOL_G // LANES):
                cs = slice(c * LANES, (c + 1) * LANES)
                u = u_ref[:, cs].astype(F32)
                before = prev(u, 1)
                after = u
                for k in (1, 2, 4)[:doublings]:
                    before = before + prev(before, k)
                    after = after + nxt(after, k)
                mixed_scr[:, cs] = ((before + after) * inv_count - u).astype(BF16)
        return fill

    for group, window in enumerate(POOL_WINDOWS):
        pl.when(gi == group)(centred_residual(window.bit_length() - 2))
    o_ref[...] = (jnp.dot(mixed_scr[...], w_ref[...], preferred_element_type=F32)
                  * sc_ref[...]).astype(o_ref.dtype)


def _pool_call(p, pool_w, pool_scale, layer, *, seq, first_blk):
    n_groups = len(POOL_WINDOWS)
    return pl.pallas_call(
        functools.partial(_pool_kernel, seq=seq),
        out_shape=jax.ShapeDtypeStruct((B * seq, POOL_W), BF16),
        grid=(B, n_groups),
        in_specs=[pl.BlockSpec((seq, POOL_G), lambda b, g: (first_blk + b, _P_POOL // POOL_G + g)),
                  pl.BlockSpec((None, None, POOL_G, POOL_G), lambda b, g: (layer, g, 0, 0)),
                  pl.BlockSpec((None, 1, POOL_G), lambda b, g: (layer, 0, g))],
        out_specs=pl.BlockSpec((seq, POOL_G), lambda b, g: (b, g)),
        scratch_shapes=[pltpu.VMEM((seq, POOL_G), BF16)],
        compiler_params=_cparams(("parallel", "parallel")),
        name="pool_branch_%d" % seq,
    )(p, pool_w, pool_scale)


def _merge_kernel(*refs, split_ctx, lat_blocks):
    if split_ctx:
        (att_ref, attc_ref, hf_ref, hb_ref, og_ref, pool_ref, poolc_ref, g0_ref, g1_ref, g2_ref,
         wa_ref, wm_ref, wp_ref, wo_ref, mlg_ref, x_ref, xc_ref, ng_ref, gate_ref, o_ref, ml_scr) = refs
        is_ctx = pl.program_id(0) >= lat_blocks
        att = jnp.where(is_ctx, attc_ref[...], att_ref[...])
        pool = jnp.where(is_ctx, poolc_ref[...], pool_ref[...])
        x = jnp.where(is_ctx, xc_ref[...], x_ref[...])
    else:
        (att_ref, hf_ref, hb_ref, og_ref, pool_ref, g0_ref, g1_ref, g2_ref,
         wa_ref, wm_ref, wp_ref, wo_ref, mlg_ref, x_ref, ng_ref, gate_ref, o_ref, ml_scr) = refs
        att, pool, x = att_ref[...], pool_ref[...], x_ref[...]

    for h in range(ML_H):
        cs = slice(h * ML_DV, (h + 1) * ML_DV)
        hs = hf_ref[:, cs].astype(F32) + hb_ref[:, cs].astype(F32)
        ms = jnp.mean(hs * hs, axis=-1, keepdims=True)
        normed = hs * lax.rsqrt(ms + EPS) * mlg_ref[:, cs]
        ml_scr[:, cs] = (normed * og_ref[:, cs].astype(F32)).astype(BF16)

    def branch(g_ref, a, w_ref):
        return g_ref[...].astype(F32) * jnp.dot(a, w_ref[...], preferred_element_type=F32)

    y = branch(g0_ref, att, wa_ref) + branch(g1_ref, ml_scr[...], wm_ref) + branch(g2_ref, pool, wp_ref)
    z = jnp.dot(y.astype(BF16), wo_ref[...], preferred_element_type=F32)
    ms = jnp.mean(z * z, axis=-1, keepdims=True)
    o_ref[...] = x + gate_ref[...] * (z * lax.rsqrt(ms + EPS) * ng_ref[...])


def _merge_call(att, att_ctx, hf, hb, p, pool, pool_ctx, w_up_att, w_up_ml, w_up_pool, w_out, ml_head_g,
                x, x_ctx, ng, gate, layer):
    tm = 256
    split_ctx = att_ctx is not None
    rows = N_TOK if split_ctx else N_LAT
    lat_blocks = N_LAT // tm

    def row_blk(width, col=0):
        return pl.BlockSpec((tm, width), lambda i: (i, col))

    def lat_blk(width):
        if not split_ctx:
            return row_blk(width)
        return pl.BlockSpec((tm, width), lambda i: (jnp.minimum(i, lat_blocks - 1), 0))

    def ctx_blk(width):
        return pl.BlockSpec((tm, width), lambda i: (jnp.maximum(i - lat_blocks, 0), 0))

    def w_blk(k):
        return pl.BlockSpec((None, k, D), lambda i: (layer, 0, 0), pipeline_mode=pl.Buffered(1))

    def with_ctx(spec_arg, ctx_spec_arg):
        return [spec_arg, ctx_spec_arg] if split_ctx else [spec_arg]

    specs_args = (
        with_ctx((lat_blk(ATT_QW), att), (ctx_blk(ATT_QW), att_ctx))
        + [(row_blk(ML_VW), hf), (row_blk(ML_VW), hb), (row_blk(ML_VW, _P_O // ML_VW), p)]
        + with_ctx((lat_blk(POOL_W), pool), (ctx_blk(POOL_W), pool_ctx))
        + [(row_blk(D, _P_GATE // D + branch), p) for branch in range(3)]
        + [(w_blk(ATT_QW), w_up_att), (w_blk(ML_VW), w_up_ml), (w_blk(POOL_W), w_up_pool), (w_blk(D), w_out),
           (pl.BlockSpec((None, 1, ML_VW), lambda i: (layer, 0, 0)), ml_head_g)]
        + with_ctx((lat_blk(D), x), (ctx_blk(D), x_ctx))
        + [(pl.BlockSpec((1, D), lambda i: (0, 0)), ng),
           (pl.BlockSpec((None, 1, D), lambda i: ((i * tm) // S, 0, 0)), gate)])

    return pl.pallas_call(
        functools.partial(_merge_kernel, split_ctx=split_ctx, lat_blocks=lat_blocks),
        out_shape=jax.ShapeDtypeStruct((rows, D), F32),
        grid=(rows // tm,),
        in_specs=[sa[0] for sa in specs_args],
        out_specs=pl.BlockSpec((tm, D), lambda i: (i, 0)),
        scratch_shapes=[pltpu.VMEM((tm, ML_VW), BF16)],
        compiler_params=_cparams(("parallel",)),
        name="merge_out",
    )(*[sa[1] for sa in specs_args])


_N_IF = 4 * ML_H
_W_IN_IF = 2 * ATT_KVW + ML_QKW + ML_VW
_W_IN_SEGMENTS = (
    (0, 2 * ATT_KVW, _P_K),
    (2 * ATT_KVW, ML_QKW, _P_MLK),
    (2 * ATT_KVW + ML_QKW, ML_VW, _P_MLV),
    (_W_IN_IF + _N_IF, ATT_QW, _P_Q),
    (_W_IN_IF + _N_IF + ATT_QW, ML_QKW, _P_MLQ),
    (_W_IN_IF + _N_IF + ATT_QW + ML_QKW, ML_VW, _P_O),
    (_W_IN_IF + _N_IF + ATT_QW + ML_QKW + ML_VW, POOL_W, _P_POOL),
    (_W_IN_IF + _N_IF + ATT_QW + ML_QKW + ML_VW + POOL_W, 3 * D, _P_GATE),
)
_RG_TILE = 512
_RG_TILES = tuple((src + c, dst + c) for src, width, dst in _W_IN_SEGMENTS for c in range(0, width, _RG_TILE))
assert all(width % _RG_TILE == 0 for _, width, _ in _W_IN_SEGMENTS) and _P_W - _P_IF == _RG_TILE


def _regroup_kernel(src_ref, dst_ref, wt_ref, o_ref):
    del src_ref, dst_ref
    t = pl.program_id(1)

    @pl.when(t < len(_RG_TILES))
    def _():
        o_ref[...] = wt_ref[0].T.astype(BF16)

    @pl.when(t == len(_RG_TILES))
    def _():
        lane = lax.broadcasted_iota(jnp.int32, (D, LANES), 1)
        gate_cols = jnp.where(lane < _N_IF, wt_ref[0, 0:LANES, :].T, 0.0).astype(BF16)
        src_l = lax.broadcasted_iota(jnp.int32, (LANES, LANES), 0)
        dst_l = lax.broadcasted_iota(jnp.int32, (LANES, LANES), 1)
        target = ((src_l >> 2) & 1) * (2 * ML_H) + (src_l >> 3) * ML_H + (src_l & (ML_H - 1))
        reorder = jnp.where((src_l < _N_IF) & (dst_l == target), 1.0, 0.0).astype(BF16)
        o_ref[:, 0:LANES] = jnp.dot(gate_cols, reorder, preferred_element_type=F32).astype(BF16)
        o_ref[:, LANES:] = jnp.zeros((D, _RG_TILE - LANES), BF16)


def _regroup_call(w_in):
    wt = jnp.swapaxes(w_in, 1, 2)
    assert all(s % _N_IF == 0 for s, _ in _RG_TILES) and _W_IN_IF % _N_IF == 0
    src = jnp.array([s // _N_IF for s, _ in _RG_TILES] + [_W_IN_IF // _N_IF], jnp.int32)
    dst = jnp.array([d // _RG_TILE for _, d in _RG_TILES] + [_P_IF // _RG_TILE], jnp.int32)
    grid_spec = pltpu.PrefetchScalarGridSpec(
        num_scalar_prefetch=2,
        grid=(DEPTH, len(_RG_TILES) + 1),
        in_specs=[pl.BlockSpec((pl.Element(1), pl.Element(_RG_TILE), pl.Element(D)),
                               lambda l, t, src, dst: (l, src[t] * _N_IF, 0))],
        out_specs=pl.BlockSpec((None, D, _RG_TILE), lambda l, t, src, dst: (l, 0, dst[t])))
    return pl.pallas_call(
        _regroup_kernel,
        out_shape=jax.ShapeDtypeStruct((DEPTH, D, _P_W), BF16),
        grid_spec=grid_spec,
        compiler_params=_cparams(("parallel", "arbitrary")),
        name="w_in_regroup",
    )(src, dst, wt)


def _rope_tables():
    t = jnp.arange(S)
    row = (t // GRID_W).astype(F32)
    colp = (t % GRID_W).astype(F32)
    n_freq = HD // 4
    inv = ROPE_THETA ** (-jnp.arange(n_freq, dtype=F32) / n_freq)
    ar, ac = row[:, None] * inv, colp[:, None] * inv
    cos_t = jnp.concatenate([jnp.cos(ar), jnp.cos(ar), jnp.cos(ac), jnp.cos(ac)], axis=1)
    sin_t = jnp.concatenate([-jnp.sin(ar), jnp.sin(ar), -jnp.sin(ac), jnp.sin(ac)], axis=1)
    return cos_t, sin_t


def kernel(x, c, ctx, c_ctx, w_mod, b_mod, norm_g, w_in, ml_gate_b, qk_norm_g, ml_head_g,
           pool_w, pool_scale, w_up_att, w_up_ml, w_up_pool, w_out, w_ffn_in, w_ffn_out):
    cos_t, sin_t = _rope_tables()
    x_lat = x.reshape(N_LAT, D)
    x_ctx = ctx.reshape(N_CTX, D)
    c8 = jnp.concatenate([c, c_ctx[None], jnp.zeros((8 - B - 1, D), F32)], axis=0)

    w_in_r = _regroup_call(w_in)
    w_out_b = w_out.astype(BF16)
    w_ffn_out_b = w_ffn_out.astype(BF16)
    w_up_att_b, w_up_ml_b, w_up_pool_b = (w.astype(BF16) for w in (w_up_att, w_up_ml, w_up_pool))
    pool_w_b = pool_w.astype(BF16)
    pool_scale_r = pool_scale.reshape(DEPTH, 1, POOL_W)
    ml_head_g_r = ml_head_g.reshape(DEPTH, 1, ML_VW)
    gate_bias = jnp.pad(ml_gate_b.reshape(DEPTH, 1, _N_IF), ((0, 0), (0, 0), (0, LANES - _N_IF)))

    assert DEPTH == 2
    xa = None
    for l in range(DEPTH):
        last = l == DEPTH - 1
        rows = N_LAT if last else N_TOK

        mod = _mod_call(c8, w_mod, b_mod, l).reshape(8, 6, 1, D)
        sh_m, sc_m, gt_m, sh_f, sc_f, gt_f = (mod[:, i] for i in range(6))

        if l == 0:
            p = _in_proj_call(x_lat, x_ctx, norm_g[l, 0][None], sh_m, sc_m, w_in_r, l)
        else:
            p = _in_proj_call(xa, None, norm_g[l, 0][None], sh_m, sc_m, w_in_r, l)

        qt, k_proc, vt = _qkprep_call(p, cos_t, sin_t, qk_norm_g[l, 0][None], qk_norm_g[l, 1][None])
        att, att_ctx = _attn_call(qt, k_proc, vt, with_ctx_queries=not last)

        hf, hb = _mlstm_call(p, _mlstm_gate_call(p, gate_bias[l]))

        pool = _pool_call(p, pool_w_b, pool_scale_r, l, seq=S, first_blk=0)
        pool_ctx = None if last else _pool_call(p, pool_w_b, pool_scale_r, l, seq=SC, first_blk=N_LAT // SC)

        xa = _merge_call(att, att_ctx, hf, hb, p, pool, pool_ctx, w_up_att_b, w_up_ml_b, w_up_pool_b,
                         w_out_b, ml_head_g_r, x_lat if l == 0 else xa, x_ctx if l == 0 else None,
                         norm_g[l, 1][None], gt_m, l)

        f = _ffn_in_call(xa, norm_g[l, 2][None], sh_f, sc_f, w_ffn_in, l, rows=rows)
        xa = _resid_call(f, w_ffn_out_b, xa, norm_g[l, 3][None], gt_f, l, rows=rows, tm=256, name="ffn_out")

    return xa.reshape(B, S, D)
```

```python
import functools
import math

import jax
import jax.numpy as jnp
from jax import lax
from jax.experimental import pallas as pl
from jax.experimental.pallas import tpu as pltpu

F32 = jnp.float32
BF16 = jnp.bfloat16

D = 2048
B = 4
S = 4096
SC = 256
DEPTH = 2
GRID_W = 64
ATT_H = 8
ATT_KVH = 2
ATT_G = ATT_H // ATT_KVH
HD = 128
ROPE_THETA = 10000.0
ML_H = 4
ML_DK = 128
ML_DV = 256
ML_L = 128
POOL_WINDOWS = (2, 4, 8, 16)
POOL_G = 256
D_FF = 5632
EPS = 1e-6

N_LAT = B * S
N_CTX = B * SC
N_TOK = N_LAT + N_CTX

ATT_QW = ATT_H * HD
ATT_KVW = ATT_KVH * HD
ML_QKW = ML_H * ML_DK
ML_VW = ML_H * ML_DV
POOL_W = len(POOL_WINDOWS) * POOL_G

_P_GATE = 0
_P_O = 3 * D
_P_POOL = _P_O + ML_VW
_P_MLV = _P_POOL + POOL_W
_P_Q = _P_MLV + ML_VW
_P_MLK = _P_Q + ATT_QW
_P_MLQ = _P_MLK + ML_QKW
_P_K = _P_MLQ + ML_QKW
_P_V = _P_K + ATT_KVW
_P_IF = _P_V + ATT_KVW
_P_W = 12288
_P_TN = 1024
_P_SIGMOID_TILES = (_P_O + ML_VW) // _P_TN
LANES = 128

VMEM_LIMIT = 56 * 1024 * 1024


def _cparams(sem, vmem=VMEM_LIMIT):
    return pltpu.CompilerParams(dimension_semantics=sem, vmem_limit_bytes=vmem)


def _sigmoid(x):
    return 0.5 * jnp.tanh(0.5 * x) + 0.5


def _mod_kernel(c_ref, w_ref, b_ref, o_ref):
    c = c_ref[...]
    sc = (c * _sigmoid(c)).astype(BF16)
    o_ref[...] = jnp.dot(sc, w_ref[...].astype(BF16), preferred_element_type=F32) + b_ref[...]


def _mod_call(c8, w_mod, b_mod, layer):
    tn = 1024
    n = w_mod.shape[2]
    return pl.pallas_call(
        _mod_kernel,
        out_shape=jax.ShapeDtypeStruct((8, n), F32),
        grid=(n // tn,),
        in_specs=[pl.BlockSpec((8, D), lambda j: (0, 0)),
                  pl.BlockSpec((None, D, tn), lambda j: (layer, 0, j)),
                  pl.BlockSpec((None, 1, tn), lambda j: (layer, 0, j))],
        out_specs=pl.BlockSpec((8, tn), lambda j: (0, j)),
        compiler_params=_cparams(("parallel",)),
        name="mod_proj",
    )(c8, w_mod, b_mod.reshape(DEPTH, 1, n))


_ROW_CHUNK = 32


def _norm_modulate_rows(x_ref, g_ref, sh_ref, sc_ref, h_ref, tm):
    gain = g_ref[...] * (1.0 + sc_ref[...])
    sh = sh_ref[...]

    def body(r, carry):
        rows = pl.ds(pl.multiple_of(r * _ROW_CHUNK, _ROW_CHUNK), _ROW_CHUNK)
        xf = x_ref[rows, :]
        ms = jnp.mean(xf * xf, axis=-1, keepdims=True)
        h_ref[rows, :] = (xf * lax.rsqrt(ms + EPS) * gain + sh).astype(BF16)
        return carry

    lax.fori_loop(0, tm // _ROW_CHUNK, body, 0, unroll=4)


def _in_proj_kernel(*refs, tm, split_ctx, ctx_state_only):
    if split_ctx:
        x_ref, xc_ref, g_ref, sh_ref, sc_ref, w_ref, o_ref, h_scr = refs
    else:
        x_ref, g_ref, sh_ref, sc_ref, w_ref, o_ref, h_scr = refs
    i = pl.program_id(0)
    j = pl.program_id(1)
    first = j == 0
    if split_ctx:
        is_ctx = i >= N_LAT // tm

        @pl.when(first & jnp.logical_not(is_ctx))
        def _():
            _norm_modulate_rows(x_ref, g_ref, sh_ref, sc_ref, h_scr, tm)

        @pl.when(first & is_ctx)
        def _():
            _norm_modulate_rows(xc_ref, g_ref, sh_ref, sc_ref, h_scr, tm)
    else:
        @pl.when(first)
        def _():
            _norm_modulate_rows(x_ref, g_ref, sh_ref, sc_ref, h_scr, tm)

    def project():
        acc = jnp.dot(h_scr[...], w_ref[...], preferred_element_type=F32)
        o_ref[...] = jnp.where(j < _P_SIGMOID_TILES, _sigmoid(acc), acc).astype(o_ref.dtype)

    if not ctx_state_only:
        project()
        return
    needed = i < N_LAT // tm
    for tile in sorted({c // _P_TN for c in (_P_MLV, _P_MLK, _P_K, _P_V, _P_IF)}):
        needed = needed | (j == tile)
    pl.when(needed)(project)

    @pl.when(jnp.logical_not(needed))
    def _():
        o_ref[...] = jnp.zeros_like(o_ref)


def _ffn_in_kernel(x_ref, g_ref, sh_ref, sc_ref, wg_ref, wu_ref, o_ref, h_scr, *, tm):
    @pl.when(pl.program_id(1) == 0)
    def _():
        _norm_modulate_rows(x_ref, g_ref, sh_ref, sc_ref, h_scr, tm)

    h = h_scr[...]
    a = jnp.dot(h, wg_ref[...].astype(BF16), preferred_element_type=F32)
    u = jnp.dot(h, wu_ref[...].astype(BF16), preferred_element_type=F32)
    o_ref[...] = (a * _sigmoid(a) * u).astype(o_ref.dtype)


def _row_specs(tm):
    return [pl.BlockSpec((tm, D), lambda i, j: (i, 0)),
            pl.BlockSpec((1, D), lambda i, j: (0, 0)),
            pl.BlockSpec((None, 1, D), lambda i, j: ((i * tm) // S, 0, 0)),
            pl.BlockSpec((None, 1, D), lambda i, j: ((i * tm) // S, 0, 0))]


def _in_proj_call(x, x_ctx, g, shift, scale, w, layer, *, ctx_state_only):
    tm, tn = 1024, _P_TN
    assert N_CTX == tm
    specs = _row_specs(tm)
    args = [x, g, shift, scale, w]
    if x_ctx is not None:
        last_lat = N_LAT // tm - 1
        specs[0] = pl.BlockSpec((tm, D), lambda i, j: (jnp.minimum(i, last_lat), 0))
        specs.insert(1, pl.BlockSpec((tm, D), lambda i, j: (0, 0), pipeline_mode=pl.Buffered(1)))
        args.insert(1, x_ctx)
    return pl.pallas_call(
        functools.partial(_in_proj_kernel, tm=tm, split_ctx=x_ctx is not None, ctx_state_only=ctx_state_only),
        out_shape=jax.ShapeDtypeStruct((N_TOK, _P_W), BF16),
        grid=(N_TOK // tm, _P_W // tn),
        in_specs=specs + [pl.BlockSpec((None, D, tn), lambda i, j: (layer, 0, j))],
        out_specs=pl.BlockSpec((tm, tn), lambda i, j: (i, j)),
        scratch_shapes=[pltpu.VMEM((tm, D), BF16)],
        compiler_params=_cparams(("parallel", "arbitrary")),
        name="in_proj",
    )(*args)


def _ffn_in_call(x, g, shift, scale, w, layer, *, rows):
    tm, tn = 1024, 512
    nj = D_FF // tn
    return pl.pallas_call(
        functools.partial(_ffn_in_kernel, tm=tm),
        out_shape=jax.ShapeDtypeStruct((rows, D_FF), BF16),
        grid=(rows // tm, nj),
        in_specs=_row_specs(tm) + [pl.BlockSpec((None, D, tn), lambda i, j: (layer, 0, j)),
                                   pl.BlockSpec((None, D, tn), lambda i, j: (layer, 0, nj + j))],
        out_specs=pl.BlockSpec((tm, tn), lambda i, j: (i, j)),
        scratch_shapes=[pltpu.VMEM((tm, D), BF16)],
        compiler_params=_cparams(("parallel", "arbitrary")),
        name="ffn_in",
    )(x, g, shift, scale, w, w)


def _resid_kernel(a_ref, w_ref, x_ref, g_ref, gate_ref, o_ref):
    z = jnp.dot(a_ref[...], w_ref[...], preferred_element_type=F32)
    ms = jnp.mean(z * z, axis=-1, keepdims=True)
    o_ref[...] = x_ref[...] + gate_ref[...] * (z * lax.rsqrt(ms + EPS) * g_ref[...])


def _resid_call(a, w, x, g, gate, layer, *, rows, tm, name):
    k = w.shape[1]
    return pl.pallas_call(
        _resid_kernel,
        out_shape=jax.ShapeDtypeStruct((rows, D), F32),
        grid=(rows // tm,),
        in_specs=[pl.BlockSpec((tm, k), lambda i: (i, 0)),
                  pl.BlockSpec((None, k, D), lambda i: (layer, 0, 0), pipeline_mode=pl.Buffered(1)),
                  pl.BlockSpec((tm, D), lambda i: (i, 0)),
                  pl.BlockSpec((1, D), lambda i: (0, 0)),
                  pl.BlockSpec((None, 1, D), lambda i: ((i * tm) // S, 0, 0))],
        out_specs=pl.BlockSpec((tm, D), lambda i: (i, 0)),
        compiler_params=_cparams(("parallel",)),
        name=name,
    )(a, w, x, g, gate)


def _head_norm_rope(xh, g, cos, sin, first_half):
    xf = xh.astype(F32)
    ms = jnp.mean(xf * xf, axis=-1, keepdims=True)
    y = xf * lax.rsqrt(ms + EPS) * g
    swapped = jnp.where(first_half, pltpu.roll(y, HD - HD // 4, 1), pltpu.roll(y, HD // 4, 1))
    return y * cos + swapped * sin


def _qkprep_kernel(q_ref, k_ref, v_ref, cos_ref, sin_ref, gq_ref, gk_ref, qt_ref, ko_ref, vt_ref,
                   *, lat_tiles):
    use_rope = pl.program_id(0) < lat_tiles
    cos = jnp.where(use_rope, cos_ref[...], 1.0)
    sin = jnp.where(use_rope, sin_ref[...], 0.0)
    lane = lax.broadcasted_iota(jnp.int32, (1, HD), 1)
    first_half = (lane % (HD // 2)) < (HD // 4)
    q_scale = HD ** -0.5 * math.log2(math.e)
    for h in range(ATT_H):
        cs = slice(h * HD, (h + 1) * HD)
        y = _head_norm_rope(q_ref[:, cs], gq_ref[...], cos, sin, first_half)
        qt_ref[cs, :] = (y * q_scale).T.astype(BF16)
    for h in range(ATT_KVH):
        cs = slice(h * HD, (h + 1) * HD)
        y = _head_norm_rope(k_ref[:, cs], gk_ref[...], cos, sin, first_half)
        ko_ref[:, cs] = y.astype(BF16)
        vt_ref[cs, :] = v_ref[:, cs].astype(F32).T.astype(BF16)


def _qkprep_call(p, cos_t, sin_t, gq, gk):
    tm = 512
    lat_tiles = N_LAT // tm
    pos_tiles = S // tm
    return pl.pallas_call(
        functools.partial(_qkprep_kernel, lat_tiles=lat_tiles),
        out_shape=(jax.ShapeDtypeStruct((ATT_QW, N_TOK), BF16),
                   jax.ShapeDtypeStruct((N_TOK, ATT_KVW), BF16),
                   jax.ShapeDtypeStruct((ATT_KVW, N_TOK), BF16)),
        grid=(N_TOK // tm,),
        in_specs=[pl.BlockSpec((tm, ATT_QW), lambda i: (i, _P_Q // ATT_QW)),
                  pl.BlockSpec((tm, ATT_KVW), lambda i: (i, _P_K // ATT_KVW)),
                  pl.BlockSpec((tm, ATT_KVW), lambda i: (i, _P_V // ATT_KVW)),
                  pl.BlockSpec((tm, HD), lambda i: (i % pos_tiles, 0)),
                  pl.BlockSpec((tm, HD), lambda i: (i % pos_tiles, 0)),
                  pl.BlockSpec((1, HD), lambda i: (0, 0)),
                  pl.BlockSpec((1, HD), lambda i: (0, 0))],
        out_specs=(pl.BlockSpec((ATT_QW, tm), lambda i: (0, i)),
                   pl.BlockSpec((tm, ATT_KVW), lambda i: (i, 0)),
                   pl.BlockSpec((ATT_KVW, tm), lambda i: (0, i))),
        compiler_params=_cparams(("parallel",)),
        name="qk_prep",
    )(p, p, p, cos_t, sin_t, gq, gk)


_TQ = 256
_KC = 1024


def _attention_heads(qt_ref, key_refs, vt_refs, o_ref, s_scr, qt_next_ref=None, m_scr=None):
    tq = qt_ref.shape[1]
    chunks = []
    base = 0
    for k_ref, vt_ref in zip(key_refs, vt_refs):
        n_keys = k_ref.shape[0]
        kc = min(_KC, n_keys)
        chunks += [(k_ref, vt_ref, c * kc, kc, base + c * kc) for c in range(n_keys // kc)]
        base += n_keys

    def fold(x, op):
        return op(x.reshape(x.shape[0] // 8, 8, tq), axis=0)

    def score_chunk(q_ref, head, slot, chunk):
        k_ref, _, r0, nr, s0 = chunk
        sc = jnp.dot(k_ref[r0:r0 + nr, :], q_ref[head * HD:(head + 1) * HD, :],
                     preferred_element_type=F32)
        s_scr[slot, s0:s0 + nr, :] = sc
        return fold(sc, jnp.max)

    def scores(q_ref, head, slot):
        mx = functools.reduce(jnp.maximum, [score_chunk(q_ref, head, slot, ch) for ch in chunks])
        return jnp.max(mx, axis=0, keepdims=True)

    def head_step(h, m, nxt):
        slot = h % 2
        mx = tot = ot = None
        for ch in chunks:
            if nxt is not None:
                part = score_chunk(nxt[0], nxt[1], 1 - slot, ch)
                mx = part if mx is None else jnp.maximum(mx, part)
            _, vt_ref, r0, nr, s0 = ch
            pr = jnp.exp2(s_scr[slot, s0:s0 + nr, :] - m)
            psum = fold(pr, jnp.sum)
            tot = psum if tot is None else tot + psum
            pv = jnp.dot(vt_ref[:, r0:r0 + nr], pr.astype(BF16), preferred_element_type=F32)
            ot = pv if ot is None else ot + pv
        denom = jnp.sum(tot, axis=0, keepdims=True)
        o_ref[:, h * HD:(h + 1) * HD] = (ot * (1.0 / denom)).T.astype(o_ref.dtype)
        return None if mx is None else jnp.max(mx, axis=0, keepdims=True)

    if qt_next_ref is None:
        m = scores(qt_ref, 0, 0)
        for h in range(ATT_G):
            m = head_step(h, m, (qt_ref, h + 1) if h + 1 < ATT_G else None)
        return

    @pl.when(pl.program_id(2) == 0)
    def _():
        m_scr[...] = jnp.broadcast_to(scores(qt_ref, 0, 0), m_scr.shape)

    m = m_scr[0:1, :]
    for h in range(ATT_G):
        m = head_step(h, m, (qt_ref, h + 1) if h + 1 < ATT_G else (qt_next_ref, 0))
    m_scr[...] = jnp.broadcast_to(m, m_scr.shape)


def _attn_latent_kernel(qt_ref, qt_next_ref, kl_ref, kc_ref, vtl_ref, vtc_ref, o_ref, s_scr, m_scr):
    _attention_heads(qt_ref, (kl_ref, kc_ref), (vtl_ref, vtc_ref), o_ref, s_scr, qt_next_ref, m_scr)


def _attn_ctx_kernel(qt_ref, kc_ref, vtc_ref, o_ref, s_scr):
    _attention_heads(qt_ref, (kc_ref,), (vtc_ref,), o_ref, s_scr)


def _attn_call(qt, k_proc, vt, *, with_ctx_queries):
    lat_tiles = S // _TQ
    ctx_blk = N_LAT // SC
    gw = ATT_G * HD

    att = pl.pallas_call(
        _attn_latent_kernel,
        out_shape=jax.ShapeDtypeStruct((N_LAT, ATT_QW), BF16),
        grid=(B, ATT_KVH, lat_tiles),
        in_specs=[pl.BlockSpec((gw, _TQ), lambda b, kv, t: (kv, b * lat_tiles + t)),
                  pl.BlockSpec((gw, _TQ),
                               lambda b, kv, t: (kv, b * lat_tiles + jnp.minimum(t + 1, lat_tiles - 1))),
                  pl.BlockSpec((S, HD), lambda b, kv, t: (b, kv)),
                  pl.BlockSpec((SC, HD), lambda b, kv, t: (ctx_blk + b, kv)),
                  pl.BlockSpec((HD, S), lambda b, kv, t: (kv, b)),
                  pl.BlockSpec((HD, SC), lambda b, kv, t: (kv, ctx_blk + b))],
        out_specs=pl.BlockSpec((_TQ, gw), lambda b, kv, t: (b * lat_tiles + t, kv)),
        scratch_shapes=[pltpu.VMEM((2, S + SC, _TQ), F32),
                        pltpu.VMEM((8, _TQ), F32)],
        compiler_params=_cparams(("parallel", "parallel", "arbitrary")),
        name="gqa_attention",
    )(qt, qt, k_proc, k_proc, vt, vt)
    if not with_ctx_queries:
        return att, None
    att_ctx = pl.pallas_call(
        _attn_ctx_kernel,
        out_shape=jax.ShapeDtypeStruct((N_CTX, ATT_QW), BF16),
        grid=(B, ATT_KVH),
        in_specs=[pl.BlockSpec((gw, SC), lambda b, kv: (kv, ctx_blk + b)),
                  pl.BlockSpec((SC, HD), lambda b, kv: (ctx_blk + b, kv)),
                  pl.BlockSpec((HD, SC), lambda b, kv: (kv, ctx_blk + b))],
        out_specs=pl.BlockSpec((SC, gw), lambda b, kv: (b, kv)),
        scratch_shapes=[pltpu.VMEM((2, SC, SC), F32)],
        compiler_params=_cparams(("parallel", "parallel")),
        name="gqa_attention_ctx",
    )(qt, k_proc, vt)
    return att, att_ctx


_ML_CHUNKS = (SC + S) // ML_L
_ML_CTX_CHUNKS = SC // ML_L
_NEG = -1e30


def _log_sigmoid(x):
    return jnp.minimum(x, 0.0) - jnp.log1p(jnp.exp(-jnp.abs(x)))


def _exact_dot_f32(x, a01):
    hi = x.astype(BF16)
    r1 = x - hi.astype(F32)
    mid = r1.astype(BF16)
    lo = (r1 - mid.astype(F32)).astype(BF16)
    return (jnp.dot(hi, a01, preferred_element_type=F32)
            + jnp.dot(mid, a01, preferred_element_type=F32)
            + jnp.dot(lo, a01, preferred_element_type=F32))


_ML_EXT = ML_DV + LANES


_GS_FIELDS = 6
_GS_CHUNKS = 8


def _mlstm_gate_kernel(g_ref, bias_ref, o_ref):
    L = ML_L
    row = lax.broadcasted_iota(jnp.int32, (L, L), 0)
    col = lax.broadcasted_iota(jnp.int32, (L, L), 1)
    lane8 = lax.broadcasted_iota(jnp.int32, (8, L), 1)
    for c in range(_GS_CHUNKS):
        gates_t = (g_ref[c * L:(c + 1) * L, :].astype(F32) + bias_ref[...]).T
        for d in range(2):
            fwd = d == 0
            tri = jnp.where((row <= col) if fwd else (row >= col), 1.0, 0.0).astype(BF16)
            r8 = gates_t[d * 8:(d + 1) * 8, :]
            b = pltpu.roll(_exact_dot_f32(_log_sigmoid(r8), tri), 4, 0)
            u = r8 - b
            g = jnp.broadcast_to(b[:, L - 1:L] if fwd else b[:, 0:1], (8, L))
            w_end = g + u
            m_loc = jnp.broadcast_to(jnp.max(w_end, axis=1, keepdims=True), (8, L))
            e = jnp.exp(w_end - m_loc)
            run_max = u
            step = 1
            while step < L:
                if fwd:
                    shifted = jnp.where(lane8 >= step, pltpu.roll(run_max, step, 1), _NEG)
                else:
                    shifted = jnp.where(lane8 < L - step, pltpu.roll(run_max, L - step, 1), _NEG)
                run_max = jnp.maximum(run_max, shifted)
                step *= 2
            for i, field in enumerate((u, e, run_max, b, g, m_loc)):
                o_ref[c, d, i * 8:(i + 1) * 8, :] = field


def _mlstm_gate_call(p, gate_bias):
    rows = _GS_CHUNKS * ML_L
    return pl.pallas_call(
        _mlstm_gate_kernel,
        out_shape=jax.ShapeDtypeStruct((N_TOK // ML_L, 2, _GS_FIELDS * 8, ML_L), F32),
        grid=(N_TOK // rows,),
        in_specs=[pl.BlockSpec((rows, LANES), lambda i: (i, _P_IF // LANES)),
                  pl.BlockSpec((1, LANES), lambda i: (0, 0))],
        out_specs=pl.BlockSpec((_GS_CHUNKS, 2, _GS_FIELDS * 8, ML_L), lambda i: (i, 0, 0, 0)),
        compiler_params=_cparams(("parallel",)),
        name="mlstm_gates",
    )(p, gate_bias)


def _mlstm_kernel(qf_ref, kf_ref, vf_ref, gsf_ref, qb_ref, kb_ref, vb_ref, gsb_ref,
                  hf_ref, hb_ref, ct_scr, m_scr):
    @pl.when(pl.program_id(1) == 0)
    def _():
        ct_scr[...] = jnp.zeros_like(ct_scr)
        m_scr[...] = jnp.zeros_like(m_scr)

    L = ML_L
    row = lax.broadcasted_iota(jnp.int32, (L, L), 0)
    col = lax.broadcasted_iota(jnp.int32, (L, L), 1)
    ones_tile = jnp.where(lax.broadcasted_iota(jnp.int32, (L, LANES), 1) == 0, 1.0, 0.0).astype(BF16)
    q_scale = ML_DK ** -0.5

    directions = []
    for d, (q_ref, k_ref, v_ref, gs_ref, h_ref) in enumerate(
            ((qf_ref, kf_ref, vf_ref, gsf_ref, hf_ref), (qb_ref, kb_ref, vb_ref, gsb_ref, hb_ref))):
        mask = (col <= row) if d == 0 else (col >= row)
        u, e, run_max, b, g, m_loc = (gs_ref[i * 8:(i + 1) * 8, :] for i in range(_GS_FIELDS))
        m_prev = m_scr[d]
        m_new = jnp.maximum(g + m_prev, m_loc)
        a_old = jnp.exp(g + m_prev - m_new)
        a_loc = jnp.exp(m_loc - m_new)
        m_scr[d] = m_new
        big_m = jnp.maximum(m_prev, run_max)
        inter = jnp.exp(m_prev - big_m)
        floor = jnp.exp(-(b + big_m))
        cols = jnp.concatenate([big_m, inter, floor, jnp.zeros((L - 24, L), F32)], axis=0).T
        directions.append((d, q_ref, k_ref, v_ref, h_ref, mask, u, e, a_old, a_loc, cols))

    for h in range(ML_H):
        for d, q_ref, k_ref, v_ref, h_ref, mask, u, e, a_old, a_loc, cols in directions:
            idx = d * ML_H + h
            ct_prev = ct_scr[idx]
            q = q_ref[:, h * ML_DK:(h + 1) * ML_DK].astype(F32)
            qs = (q * q_scale).astype(BF16)
            qi = (q * (cols[:, 8 + h:9 + h] * q_scale)).astype(BF16)
            kt = k_ref[:, h * ML_DK:(h + 1) * ML_DK].astype(F32).T
            v_ext = jnp.concatenate([v_ref[:, h * ML_DV:(h + 1) * ML_DV], ones_tile], axis=1)

            qk = jnp.dot(qs, kt.astype(BF16), preferred_element_type=F32)
            expo = jnp.where(mask, u[h:h + 1, :] - cols[:, h:h + 1], _NEG)
            w = (jnp.exp(expo) * qk).astype(BF16)
            nd = jnp.dot(jnp.concatenate([w, qi], axis=1),
                         jnp.concatenate([v_ext, ct_prev.astype(BF16)], axis=0),
                         preferred_element_type=F32)
            den = nd[:, ML_DV:ML_DV + 1]
            inv = 1.0 / jnp.maximum(jnp.abs(den), cols[:, 16 + h:17 + h])
            h_ref[:, h * ML_DV:(h + 1) * ML_DV] = (nd[:, :ML_DV] * inv).astype(h_ref.dtype)

            c_loc = jnp.dot((kt * e[h:h + 1, :]).astype(BF16), v_ext, preferred_element_type=F32)
            keep = jnp.concatenate([a_old[h:h + 1, :]] * (_ML_EXT // LANES), axis=1)
            gain = jnp.concatenate([a_loc[h:h + 1, :]] * (_ML_EXT // LANES), axis=1)
            ct_scr[idx] = keep * ct_prev + gain * c_loc


def _mlstm_call(p, gate_stats):
    lat_chunks = S // ML_L
    ctx_blk = N_LAT // ML_L

    def fwd_rb(b, t):
        return jnp.where(t < _ML_CTX_CHUNKS, ctx_blk + b * _ML_CTX_CHUNKS + t,
                         b * lat_chunks + t - _ML_CTX_CHUNKS)

    def rev_rb(b, t):
        return jnp.where(t < _ML_CTX_CHUNKS, ctx_blk + b * _ML_CTX_CHUNKS + (_ML_CTX_CHUNKS - 1 - t),
                         b * lat_chunks + (_ML_CHUNKS - 1 - t))

    def specs(rb, d):
        return [pl.BlockSpec((ML_L, ML_QKW), lambda b, t: (rb(b, t), _P_MLQ // ML_QKW)),
                pl.BlockSpec((ML_L, ML_QKW), lambda b, t: (rb(b, t), _P_MLK // ML_QKW)),
                pl.BlockSpec((ML_L, ML_VW), lambda b, t: (rb(b, t), _P_MLV // ML_VW)),
                pl.BlockSpec((None, None, _GS_FIELDS * 8, ML_L), lambda b, t: (rb(b, t), d, 0, 0))]

    n_chain = 2 * ML_H
    return pl.pallas_call(
        _mlstm_kernel,
        out_shape=(jax.ShapeDtypeStruct((N_TOK, ML_VW), BF16),
                   jax.ShapeDtypeStruct((N_TOK, ML_VW), BF16)),
        grid=(B, _ML_CHUNKS),
        in_specs=specs(fwd_rb, 0) + specs(rev_rb, 1),
        out_specs=(pl.BlockSpec((ML_L, ML_VW), lambda b, t: (fwd_rb(b, t), 0)),
                   pl.BlockSpec((ML_L, ML_VW), lambda b, t: (rev_rb(b, t), 0))),
        scratch_shapes=[pltpu.VMEM((n_chain, ML_DK, _ML_EXT), F32),
                        pltpu.VMEM((2, 8, ML_L), F32)],
        compiler_params=_cparams(("parallel", "arbitrary")),
        name="mlstm",
    )(p, p, p, gate_stats, p, p, p, gate_stats)


def _pool_kernel(u_ref, w_ref, sc_ref, o_ref, mixed_scr, *, seq):
    gi = pl.program_id(1)
    t = lax.broadcasted_iota(jnp.int32, (seq, 1), 0)

    def prev(x, k):
        return jnp.where(t >= k, pltpu.roll(x, k, 0), 0.0)

    def nxt(x, k):
        return jnp.where(t < seq - k, pltpu.roll(x, seq - k, 0), 0.0)

    def centred_residual(doublings):
        def fill():
            half_w = 2 ** doublings
            inv_count = 1.0 / (jnp.minimum(t + half_w, seq) - jnp.maximum(t - half_w, 0)).astype(F32)
            for c in range(POOL_G // LANES):
                cs = slice(c * LANES, (c + 1) * LANES)
                u = u_ref[:, cs].astype(F32)
                before = prev(u, 1)
                after = u
                for k in (1, 2, 4)[:doublings]:
                    before = before + prev(before, k)
                    after = after + nxt(after, k)
                mixed_scr[:, cs] = ((before + after) * inv_count - u).astype(BF16)
        return fill

    for group, window in enumerate(POOL_WINDOWS):
        pl.when(gi == group)(centred_residual(window.bit_length() - 2))
    o_ref[...] = (jnp.dot(mixed_scr[...], w_ref[...], preferred_element_type=F32)
                  * sc_ref[...]).astype(o_ref.dtype)


def _pool_call(p, pool_w, pool_scale, layer, *, seq, first_blk):
    n_groups = len(POOL_WINDOWS)
    return pl.pallas_call(
        functools.partial(_pool_kernel, seq=seq),
        out_shape=jax.ShapeDtypeStruct((B * seq, POOL_W), BF16),
        grid=(B, n_groups),
        in_specs=[pl.BlockSpec((seq, POOL_G), lambda b, g: (first_blk + b, _P_POOL // POOL_G + g)),
                  pl.BlockSpec((None, None, POOL_G, POOL_G), lambda b, g: (layer, g, 0, 0)),
                  pl.BlockSpec((None, 1, POOL_G), lambda b, g: (layer, 0, g))],
        out_specs=pl.BlockSpec((seq, POOL_G), lambda b, g: (b, g)),
        scratch_shapes=[pltpu.VMEM((seq, POOL_G), BF16)],
        compiler_params=_cparams(("parallel", "parallel")),
        name="pool_branch_%d" % seq,
    )(p, pool_w, pool_scale)


def _merge_kernel(*refs, split_ctx, lat_blocks):
    if split_ctx:
        (att_ref, attc_ref, hf_ref, hb_ref, og_ref, pool_ref, poolc_ref, g0_ref, g1_ref, g2_ref,
         wa_ref, wm_ref, wp_ref, wo_ref, mlg_ref, x_ref, xc_ref, ng_ref, gate_ref, o_ref, ml_scr) = refs
        is_ctx = pl.program_id(0) >= lat_blocks
        att = jnp.where(is_ctx, attc_ref[...], att_ref[...])
        pool = jnp.where(is_ctx, poolc_ref[...], pool_ref[...])
        x = jnp.where(is_ctx, xc_ref[...], x_ref[...])
    else:
        (att_ref, hf_ref, hb_ref, og_ref, pool_ref, g0_ref, g1_ref, g2_ref,
         wa_ref, wm_ref, wp_ref, wo_ref, mlg_ref, x_ref, ng_ref, gate_ref, o_ref, ml_scr) = refs
        att, pool, x = att_ref[...], pool_ref[...], x_ref[...]

    for h in range(ML_H):
        cs = slice(h * ML_DV, (h + 1) * ML_DV)
        hs = hf_ref[:, cs].astype(F32) + hb_ref[:, cs].astype(F32)
        ms = jnp.mean(hs * hs, axis=-1, keepdims=True)
        normed = hs * lax.rsqrt(ms + EPS) * mlg_ref[:, cs]
        ml_scr[:, cs] = (normed * og_ref[:, cs].astype(F32)).astype(BF16)

    def branch(g_ref, a, w_ref):
        return g_ref[...].astype(F32) * jnp.dot(a, w_ref[...], preferred_element_type=F32)

    y = branch(g0_ref, att, wa_ref) + branch(g1_ref, ml_scr[...], wm_ref) + branch(g2_ref, pool, wp_ref)
    z = jnp.dot(y.astype(BF16), wo_ref[...], preferred_element_type=F32)
    ms = jnp.mean(z * z, axis=-1, keepdims=True)
    o_ref[...] = x + gate_ref[...] * (z * lax.rsqrt(ms + EPS) * ng_ref[...])


def _merge_call(att, att_ctx, hf, hb, p, pool, pool_ctx, w_up_att, w_up_ml, w_up_pool, w_out, ml_head_g,
                x, x_ctx, ng, gate, layer):
    tm = 256
    split_ctx = att_ctx is not None
    rows = N_TOK if split_ctx else N_LAT
    lat_blocks = N_LAT // tm

    def row_blk(width, col=0):
        return pl.BlockSpec((tm, width), lambda i: (i, col))

    def lat_blk(width):
        if not split_ctx:
            return row_blk(width)
        return pl.BlockSpec((tm, width), lambda i: (jnp.minimum(i, lat_blocks - 1), 0))

    def ctx_blk(width):
        return pl.BlockSpec((tm, width), lambda i: (jnp.maximum(i - lat_blocks, 0), 0))

    def w_blk(k):
        return pl.BlockSpec((None, k, D), lambda i: (layer, 0, 0), pipeline_mode=pl.Buffered(1))

    def with_ctx(spec_arg, ctx_spec_arg):
        return [spec_arg, ctx_spec_arg] if split_ctx else [spec_arg]

    specs_args = (
        with_ctx((lat_blk(ATT_QW), att), (ctx_blk(ATT_QW), att_ctx))
        + [(row_blk(ML_VW), hf), (row_blk(ML_VW), hb), (row_blk(ML_VW, _P_O // ML_VW), p)]
        + with_ctx((lat_blk(POOL_W), pool), (ctx_blk(POOL_W), pool_ctx))
        + [(row_blk(D, _P_GATE // D + branch), p) for branch in range(3)]
        + [(w_blk(ATT_QW), w_up_att), (w_blk(ML_VW), w_up_ml), (w_blk(POOL_W), w_up_pool), (w_blk(D), w_out),
           (pl.BlockSpec((None, 1, ML_VW), lambda i: (layer, 0, 0)), ml_head_g)]
        + with_ctx((lat_blk(D), x), (ctx_blk(D), x_ctx))
        + [(pl.BlockSpec((1, D), lambda i: (0, 0)), ng),
           (pl.BlockSpec((None, 1, D), lambda i: ((i * tm) // S, 0, 0)), gate)])

    return pl.pallas_call(
        functools.partial(_merge_kernel, split_ctx=split_ctx, lat_blocks=lat_blocks),
        out_shape=jax.ShapeDtypeStruct((rows, D), F32),
        grid=(rows // tm,),
        in_specs=[sa[0] for sa in specs_args],
        out_specs=pl.BlockSpec((tm, D), lambda i: (i, 0)),
        scratch_shapes=[pltpu.VMEM((tm, ML_VW), BF16)],
        compiler_params=_cparams(("parallel",)),
        name="merge_out",
    )(*[sa[1] for sa in specs_args])


_N_IF = 4 * ML_H
_W_IN_IF = 2 * ATT_KVW + ML_QKW + ML_VW
_W_IN_SEGMENTS = (
    (0, 2 * ATT_KVW, _P_K),
    (2 * ATT_KVW, ML_QKW, _P_MLK),
    (2 * ATT_KVW + ML_QKW, ML_VW, _P_MLV),
    (_W_IN_IF + _N_IF, ATT_QW, _P_Q),
    (_W_IN_IF + _N_IF + ATT_QW, ML_QKW, _P_MLQ),
    (_W_IN_IF + _N_IF + ATT_QW + ML_QKW, ML_VW, _P_O),
    (_W_IN_IF + _N_IF + ATT_QW + ML_QKW + ML_VW, POOL_W, _P_POOL),
    (_W_IN_IF + _N_IF + ATT_QW + ML_QKW + ML_VW + POOL_W, 3 * D, _P_GATE),
)
_RG_TILE = 512
_RG_TILES = tuple((src + c, dst + c) for src, width, dst in _W_IN_SEGMENTS for c in range(0, width, _RG_TILE))
assert all(width % _RG_TILE == 0 for _, width, _ in _W_IN_SEGMENTS) and _P_W - _P_IF == _RG_TILE


def _regroup_kernel(src_ref, dst_ref, wt_ref, o_ref):
    del src_ref, dst_ref
    t = pl.program_id(1)

    @pl.when(t < len(_RG_TILES))
    def _():
        o_ref[...] = wt_ref[0].T.astype(BF16)

    @pl.when(t == len(_RG_TILES))
    def _():
        lane = lax.broadcasted_iota(jnp.int32, (D, LANES), 1)
        gate_cols = jnp.where(lane < _N_IF, wt_ref[0, 0:LANES, :].T, 0.0).astype(BF16)
        src_l = lax.broadcasted_iota(jnp.int32, (LANES, LANES), 0)
        dst_l = lax.broadcasted_iota(jnp.int32, (LANES, LANES), 1)
        target = ((src_l >> 2) & 1) * (2 * ML_H) + (src_l >> 3) * ML_H + (src_l & (ML_H - 1))
        reorder = jnp.where((src_l < _N_IF) & (dst_l == target), 1.0, 0.0).astype(BF16)
        o_ref[:, 0:LANES] = jnp.dot(gate_cols, reorder, preferred_element_type=F32).astype(BF16)
        o_ref[:, LANES:] = jnp.zeros((D, _RG_TILE - LANES), BF16)


def _regroup_call(w_in):
    wt = jnp.swapaxes(w_in, 1, 2)
    assert all(s % _N_IF == 0 for s, _ in _RG_TILES) and _W_IN_IF % _N_IF == 0
    src = jnp.array([s // _N_IF for s, _ in _RG_TILES] + [_W_IN_IF // _N_IF], jnp.int32)
    dst = jnp.array([d // _RG_TILE for _, d in _RG_TILES] + [_P_IF // _RG_TILE], jnp.int32)
    grid_spec = pltpu.PrefetchScalarGridSpec(
        num_scalar_prefetch=2,
        grid=(DEPTH, len(_RG_TILES) + 1),
        in_specs=[pl.BlockSpec((pl.Element(1), pl.Element(_RG_TILE), pl.Element(D)),
                               lambda l, t, src, dst: (l, src[t] * _N_IF, 0))],
        out_specs=pl.BlockSpec((None, D, _RG_TILE), lambda l, t, src, dst: (l, 0, dst[t])))
    return pl.pallas_call(
        _regroup_kernel,
        out_shape=jax.ShapeDtypeStruct((DEPTH, D, _P_W), BF16),
        grid_spec=grid_spec,
        compiler_params=_cparams(("parallel", "arbitrary")),
        name="w_in_regroup",
    )(src, dst, wt)


def _rope_tables():
    t = jnp.arange(S)
    row = (t // GRID_W).astype(F32)
    colp = (t % GRID_W).astype(F32)
    n_freq = HD // 4
    inv = ROPE_THETA ** (-jnp.arange(n_freq, dtype=F32) / n_freq)
    ar, ac = row[:, None] * inv, colp[:, None] * inv
    cos_t = jnp.concatenate([jnp.cos(ar), jnp.cos(ar), jnp.cos(ac), jnp.cos(ac)], axis=1)
    sin_t = jnp.concatenate([-jnp.sin(ar), jnp.sin(ar), -jnp.sin(ac), jnp.sin(ac)], axis=1)
    return cos_t, sin_t


def kernel(x, c, ctx, c_ctx, w_mod, b_mod, norm_g, w_in, ml_gate_b, qk_norm_g, ml_head_g,
           pool_w, pool_scale, w_up_att, w_up_ml, w_up_pool, w_out, w_ffn_in, w_ffn_out):
    cos_t, sin_t = _rope_tables()
    x_lat = x.reshape(N_LAT, D)
    x_ctx = ctx.reshape(N_CTX, D)
    c8 = jnp.concatenate([c, c_ctx[None], jnp.zeros((8 - B - 1, D), F32)], axis=0)

    w_in_r = _regroup_call(w_in)
    w_out_b = w_out.astype(BF16)
    w_ffn_out_b = w_ffn_out.astype(BF16)
    w_up_att_b, w_up_ml_b, w_up_pool_b = (w.astype(BF16) for w in (w_up_att, w_up_ml, w_up_pool))
    pool_w_b = pool_w.astype(BF16)
    pool_scale_r = pool_scale.reshape(DEPTH, 1, POOL_W)
    ml_head_g_r = ml_head_g.reshape(DEPTH, 1, ML_VW)
    gate_bias = jnp.pad(ml_gate_b.reshape(DEPTH, 1, _N_IF), ((0, 0), (0, 0), (0, LANES - _N_IF)))

    assert DEPTH == 2
    xa = None
    for l in range(DEPTH):
        last = l == DEPTH - 1
        rows = N_LAT if last else N_TOK

        mod = _mod_call(c8, w_mod, b_mod, l).reshape(8, 6, 1, D)
        sh_m, sc_m, gt_m, sh_f, sc_f, gt_f = (mod[:, i] for i in range(6))

        if l == 0:
            p = _in_proj_call(x_lat, x_ctx, norm_g[l, 0][None], sh_m, sc_m, w_in_r, l, ctx_state_only=last)
        else:
            p = _in_proj_call(xa, None, norm_g[l, 0][None], sh_m, sc_m, w_in_r, l, ctx_state_only=last)

        qt, k_proc, vt = _qkprep_call(p, cos_t, sin_t, qk_norm_g[l, 0][None], qk_norm_g[l, 1][None])
        att, att_ctx = _attn_call(qt, k_proc, vt, with_ctx_queries=not last)

        hf, hb = _mlstm_call(p, _mlstm_gate_call(p, gate_bias[l]))

        pool = _pool_call(p, pool_w_b, pool_scale_r, l, seq=S, first_blk=0)
        pool_ctx = None if last else _pool_call(p, pool_w_b, pool_scale_r, l, seq=SC, first_blk=N_LAT // SC)

        xa = _merge_call(att, att_ctx, hf, hb, p, pool, pool_ctx, w_up_att_b, w_up_ml_b, w_up_pool_b,
                         w_out_b, ml_head_g_r, x_lat if l == 0 else xa, x_ctx if l == 0 else None,
                         norm_g[l, 1][None], gt_m, l)

        f = _ffn_in_call(xa, norm_g[l, 2][None], sh_f, sc_f, w_ffn_in, l, rows=rows)
        xa = _resid_call(f, w_ffn_out_b, xa, norm_g[l, 3][None], gt_f, l, rows=rows, tm=256, name="ffn_out")

    return xa.reshape(B, S, D)
```

```python
import functools
import math

import jax
import jax.numpy as jnp
from jax import lax
from jax.experimental import pallas as pl
from jax.experimental.pallas import tpu as pltpu

F32 = jnp.float32
BF16 = jnp.bfloat16

D = 2048
B = 4
S = 4096
SC = 256
DEPTH = 2
GRID_W = 64
ATT_H = 8
ATT_KVH = 2
ATT_G = ATT_H // ATT_KVH
HD = 128
ROPE_THETA = 10000.0
ML_H = 4
ML_DK = 128
ML_DV = 256
ML_L = 128
POOL_WINDOWS = (2, 4, 8, 16)
POOL_G = 256
D_FF = 5632
EPS = 1e-6

N_LAT = B * S
N_CTX = B * SC
N_TOK = N_LAT + N_CTX

ATT_QW = ATT_H * HD
ATT_KVW = ATT_KVH * HD
ML_QKW = ML_H * ML_DK
ML_VW = ML_H * ML_DV
POOL_W = len(POOL_WINDOWS) * POOL_G

_P_GATE = 0
_P_O = 3 * D
_P_POOL = _P_O + ML_VW
_P_MLV = _P_POOL + POOL_W
_P_Q = _P_MLV + ML_VW
_P_MLK = _P_Q + ATT_QW
_P_MLQ = _P_MLK + ML_QKW
_P_K = _P_MLQ + ML_QKW
_P_V = _P_K + ATT_KVW
_P_IF = _P_V + ATT_KVW
_P_W = 12288
_P_TN = 1024
_P_SIGMOID_TILES = (_P_O + ML_VW) // _P_TN
LANES = 128

VMEM_LIMIT = 56 * 1024 * 1024


def _cparams(sem, vmem=VMEM_LIMIT):
    return pltpu.CompilerParams(dimension_semantics=sem, vmem_limit_bytes=vmem)


def _sigmoid(x):
    return 0.5 * jnp.tanh(0.5 * x) + 0.5


def _mod_kernel(c_ref, w_ref, b_ref, o_ref):
    c = c_ref[...]
    sc = (c * _sigmoid(c)).astype(BF16)
    o_ref[...] = jnp.dot(sc, w_ref[...].astype(BF16), preferred_element_type=F32) + b_ref[...]


def _mod_call(c8, w_mod, b_mod, layer):
    tn = 1024
    n = w_mod.shape[2]
    return pl.pallas_call(
        _mod_kernel,
        out_shape=jax.ShapeDtypeStruct((8, n), F32),
        grid=(n // tn,),
        in_specs=[pl.BlockSpec((8, D), lambda j: (0, 0)),
                  pl.BlockSpec((None, D, tn), lambda j: (layer, 0, j)),
                  pl.BlockSpec((None, 1, tn), lambda j: (layer, 0, j))],
        out_specs=pl.BlockSpec((8, tn), lambda j: (0, j)),
        compiler_params=_cparams(("parallel",)),
        name="mod_proj",
    )(c8, w_mod, b_mod.reshape(DEPTH, 1, n))


_ROW_CHUNK = 32


def _norm_modulate_rows(x_ref, g_ref, sh_ref, sc_ref, h_ref, tm):
    gain = g_ref[...] * (1.0 + sc_ref[...])
    sh = sh_ref[...]

    def body(r, carry):
        rows = pl.ds(pl.multiple_of(r * _ROW_CHUNK, _ROW_CHUNK), _ROW_CHUNK)
        xf = x_ref[rows, :]
        ms = jnp.mean(xf * xf, axis=-1, keepdims=True)
        h_ref[rows, :] = (xf * lax.rsqrt(ms + EPS) * gain + sh).astype(BF16)
        return carry

    lax.fori_loop(0, tm // _ROW_CHUNK, body, 0, unroll=4)


def _in_proj_kernel(*refs, tm, split_ctx, ctx_state_only):
    if split_ctx:
        x_ref, xc_ref, g_ref, sh_ref, sc_ref, w_ref, o_ref, h_scr = refs
    else:
        x_ref, g_ref, sh_ref, sc_ref, w_ref, o_ref, h_scr = refs
    i = pl.program_id(0)
    j = pl.program_id(1)
    first = j == 0
    if split_ctx:
        is_ctx = i >= N_LAT // tm

        @pl.when(first & jnp.logical_not(is_ctx))
        def _():
            _norm_modulate_rows(x_ref, g_ref, sh_ref, sc_ref, h_scr, tm)

        @pl.when(first & is_ctx)
        def _():
            _norm_modulate_rows(xc_ref, g_ref, sh_ref, sc_ref, h_scr, tm)
    else:
        @pl.when(first)
        def _():
            _norm_modulate_rows(x_ref, g_ref, sh_ref, sc_ref, h_scr, tm)

    def project():
        acc = jnp.dot(h_scr[...], w_ref[...], preferred_element_type=F32)
        o_ref[...] = jnp.where(j < _P_SIGMOID_TILES, _sigmoid(acc), acc).astype(o_ref.dtype)

    if not ctx_state_only:
        project()
        return
    needed = i < N_LAT // tm
    for tile in sorted({c // _P_TN for c in (_P_MLV, _P_MLK, _P_K, _P_V, _P_IF)}):
        needed = needed | (j == tile)
    pl.when(needed)(project)

    @pl.when(jnp.logical_not(needed))
    def _():
        o_ref[...] = jnp.zeros_like(o_ref)


def _ffn_in_kernel(x_ref, g_ref, sh_ref, sc_ref, wg_ref, wu_ref, o_ref, h_scr, *, tm):
    @pl.when(pl.program_id(1) == 0)
    def _():
        _norm_modulate_rows(x_ref, g_ref, sh_ref, sc_ref, h_scr, tm)

    h = h_scr[...]
    a = jnp.dot(h, wg_ref[...].astype(BF16), preferred_element_type=F32)
    u = jnp.dot(h, wu_ref[...].astype(BF16), preferred_element_type=F32)
    o_ref[...] = (a * _sigmoid(a) * u).astype(o_ref.dtype)


def _row_specs(tm):
    return [pl.BlockSpec((tm, D), lambda i, j: (i, 0)),
            pl.BlockSpec((1, D), lambda i, j: (0, 0)),
            pl.BlockSpec((None, 1, D), lambda i, j: ((i * tm) // S, 0, 0)),
            pl.BlockSpec((None, 1, D), lambda i, j: ((i * tm) // S, 0, 0))]


def _in_proj_call(x, x_ctx, g, shift, scale, w, layer, *, ctx_state_only):
    tm, tn = 1024, _P_TN
    assert N_CTX == tm
    specs = _row_specs(tm)
    args = [x, g, shift, scale, w]
    if x_ctx is not None:
        last_lat = N_LAT // tm - 1
        specs[0] = pl.BlockSpec((tm, D), lambda i, j: (jnp.minimum(i, last_lat), 0))
        specs.insert(1, pl.BlockSpec((tm, D), lambda i, j: (0, 0), pipeline_mode=pl.Buffered(1)))
        args.insert(1, x_ctx)
    return pl.pallas_call(
        functools.partial(_in_proj_kernel, tm=tm, split_ctx=x_ctx is not None, ctx_state_only=ctx_state_only),
        out_shape=jax.ShapeDtypeStruct((N_TOK, _P_W), BF16),
        grid=(N_TOK // tm, _P_W // tn),
        in_specs=specs + [pl.BlockSpec((None, D, tn), lambda i, j: (layer, 0, j))],
        out_specs=pl.BlockSpec((tm, tn), lambda i, j: (i, j)),
        scratch_shapes=[pltpu.VMEM((tm, D), BF16)],
        compiler_params=_cparams(("parallel", "arbitrary")),
        name="in_proj",
    )(*args)


def _ffn_in_call(x, g, shift, scale, w, layer, *, rows):
    tm, tn = 1024, 512
    nj = D_FF // tn
    return pl.pallas_call(
        functools.partial(_ffn_in_kernel, tm=tm),
        out_shape=jax.ShapeDtypeStruct((rows, D_FF), BF16),
        grid=(rows // tm, nj),
        in_specs=_row_specs(tm) + [pl.BlockSpec((None, D, tn), lambda i, j: (layer, 0, j)),
                                   pl.BlockSpec((None, D, tn), lambda i, j: (layer, 0, nj + j))],
        out_specs=pl.BlockSpec((tm, tn), lambda i, j: (i, j)),
        scratch_shapes=[pltpu.VMEM((tm, D), BF16)],
        compiler_params=_cparams(("parallel", "arbitrary")),
        name="ffn_in",
    )(x, g, shift, scale, w, w)


def _resid_kernel(a_ref, w_ref, x_ref, g_ref, gate_ref, o_ref):
    z = jnp.dot(a_ref[...], w_ref[...], preferred_element_type=F32)
    ms = jnp.mean(z * z, axis=-1, keepdims=True)
    o_ref[...] = x_ref[...] + gate_ref[...] * (z * lax.rsqrt(ms + EPS) * g_ref[...])


def _resid_call(a, w, x, g, gate, layer, *, rows, tm, name):
    k = w.shape[1]
    return pl.pallas_call(
        _resid_kernel,
        out_shape=jax.ShapeDtypeStruct((rows, D), F32),
        grid=(rows // tm,),
        in_specs=[pl.BlockSpec((tm, k), lambda i: (i, 0)),
                  pl.BlockSpec((None, k, D), lambda i: (layer, 0, 0), pipeline_mode=pl.Buffered(1)),
                  pl.BlockSpec((tm, D), lambda i: (i, 0)),
                  pl.BlockSpec((1, D), lambda i: (0, 0)),
                  pl.BlockSpec((None, 1, D), lambda i: ((i * tm) // S, 0, 0))],
        out_specs=pl.BlockSpec((tm, D), lambda i: (i, 0)),
        compiler_params=_cparams(("parallel",)),
        name=name,
    )(a, w, x, g, gate)


def _head_norm_rope(xh, g, cos, sin, swap_pairs):
    xf = xh.astype(F32)
    ms = jnp.mean(xf * xf, axis=-1, keepdims=True)
    y = xf * lax.rsqrt(ms + EPS) * g
    swapped = jnp.dot(y.astype(BF16), swap_pairs, preferred_element_type=F32)
    return y * cos + swapped * sin


def _qkprep_kernel(q_ref, k_ref, v_ref, cos_ref, sin_ref, gq_ref, gk_ref, qt_ref, ko_ref, vt_ref,
                   *, lat_tiles):
    use_rope = pl.program_id(0) < lat_tiles
    cos = jnp.where(use_rope, cos_ref[...], 1.0)
    sin = jnp.where(use_rope, sin_ref[...], 0.0)
    src = lax.broadcasted_iota(jnp.int32, (HD, HD), 0)
    dst = lax.broadcasted_iota(jnp.int32, (HD, HD), 1)
    partner = jnp.where((dst % (HD // 2)) < (HD // 4), dst + HD // 4, dst - HD // 4)
    swap_pairs = jnp.where(src == partner, 1.0, 0.0).astype(BF16)
    q_scale = HD ** -0.5 * math.log2(math.e)
    for h in range(ATT_H):
        cs = slice(h * HD, (h + 1) * HD)
        y = _head_norm_rope(q_ref[:, cs], gq_ref[...], cos, sin, swap_pairs)
        qt_ref[cs, :] = (y * q_scale).T.astype(BF16)
    for h in range(ATT_KVH):
        cs = slice(h * HD, (h + 1) * HD)
        y = _head_norm_rope(k_ref[:, cs], gk_ref[...], cos, sin, swap_pairs)
        ko_ref[:, cs] = y.astype(BF16)
        vt_ref[cs, :] = v_ref[:, cs].astype(F32).T.astype(BF16)


def _qkprep_call(p, cos_t, sin_t, gq, gk):
    tm = 512
    lat_tiles = N_LAT // tm
    pos_tiles = S // tm
    return pl.pallas_call(
        functools.partial(_qkprep_kernel, lat_tiles=lat_tiles),
        out_shape=(jax.ShapeDtypeStruct((ATT_QW, N_TOK), BF16),
                   jax.ShapeDtypeStruct((N_TOK, ATT_KVW), BF16),
                   jax.ShapeDtypeStruct((ATT_KVW, N_TOK), BF16)),
        grid=(N_TOK // tm,),
        in_specs=[pl.BlockSpec((tm, ATT_QW), lambda i: (i, _P_Q // ATT_QW)),
                  pl.BlockSpec((tm, ATT_KVW), lambda i: (i, _P_K // ATT_KVW)),
                  pl.BlockSpec((tm, ATT_KVW), lambda i: (i, _P_V // ATT_KVW)),
                  pl.BlockSpec((tm, HD), lambda i: (i % pos_tiles, 0)),
                  pl.BlockSpec((tm, HD), lambda i: (i % pos_tiles, 0)),
                  pl.BlockSpec((1, HD), lambda i: (0, 0)),
                  pl.BlockSpec((1, HD), lambda i: (0, 0))],
        out_specs=(pl.BlockSpec((ATT_QW, tm), lambda i: (0, i)),
                   pl.BlockSpec((tm, ATT_KVW), lambda i: (i, 0)),
                   pl.BlockSpec((ATT_KVW, tm), lambda i: (0, i))),
        compiler_params=_cparams(("parallel",)),
        name="qk_prep",
    )(p, p, p, cos_t, sin_t, gq, gk)


_TQ = 256
_KC = 1024


def _attention_heads(qt_ref, key_refs, vt_refs, o_ref, s_scr, qt_next_ref=None, m_scr=None):
    tq = qt_ref.shape[1]
    chunks = []
    base = 0
    for k_ref, vt_ref in zip(key_refs, vt_refs):
        n_keys = k_ref.shape[0]
        kc = min(_KC, n_keys)
        chunks += [(k_ref, vt_ref, c * kc, kc, base + c * kc) for c in range(n_keys // kc)]
        base += n_keys

    def fold(x, op):
        return op(x.reshape(x.shape[0] // 8, 8, tq), axis=0)

    def score_chunk(q_ref, head, slot, chunk):
        k_ref, _, r0, nr, s0 = chunk
        sc = jnp.dot(k_ref[r0:r0 + nr, :], q_ref[head * HD:(head + 1) * HD, :],
                     preferred_element_type=F32)
        s_scr[slot, s0:s0 + nr, :] = sc
        return fold(sc, jnp.max)

    def scores(q_ref, head, slot):
        mx = functools.reduce(jnp.maximum, [score_chunk(q_ref, head, slot, ch) for ch in chunks])
        return jnp.max(mx, axis=0, keepdims=True)

    def head_step(h, m, nxt):
        slot = h % 2
        mx = tot = ot = None
        for ch in chunks:
            if nxt is not None:
                part = score_chunk(nxt[0], nxt[1], 1 - slot, ch)
                mx = part if mx is None else jnp.maximum(mx, part)
            _, vt_ref, r0, nr, s0 = ch
            pr = jnp.exp2(s_scr[slot, s0:s0 + nr, :] - m)
            psum = fold(pr, jnp.sum)
            tot = psum if tot is None else tot + psum
            pv = jnp.dot(vt_ref[:, r0:r0 + nr], pr.astype(BF16), preferred_element_type=F32)
            ot = pv if ot is None else ot + pv
        denom = jnp.sum(tot, axis=0, keepdims=True)
        o_ref[:, h * HD:(h + 1) * HD] = (ot * (1.0 / denom)).T.astype(o_ref.dtype)
        return None if mx is None else jnp.max(mx, axis=0, keepdims=True)

    if qt_next_ref is None:
        m = scores(qt_ref, 0, 0)
        for h in range(ATT_G):
            m = head_step(h, m, (qt_ref, h + 1) if h + 1 < ATT_G else None)
        return

    @pl.when(pl.program_id(2) == 0)
    def _():
        m_scr[...] = jnp.broadcast_to(scores(qt_ref, 0, 0), m_scr.shape)

    m = m_scr[0:1, :]
    for h in range(ATT_G):
        m = head_step(h, m, (qt_ref, h + 1) if h + 1 < ATT_G else (qt_next_ref, 0))
    m_scr[...] = jnp.broadcast_to(m, m_scr.shape)


def _attn_latent_kernel(qt_ref, qt_next_ref, kl_ref, kc_ref, vtl_ref, vtc_ref, o_ref, s_scr, m_scr):
    _attention_heads(qt_ref, (kl_ref, kc_ref), (vtl_ref, vtc_ref), o_ref, s_scr, qt_next_ref, m_scr)


def _attn_ctx_kernel(qt_ref, kc_ref, vtc_ref, o_ref, s_scr):
    _attention_heads(qt_ref, (kc_ref,), (vtc_ref,), o_ref, s_scr)


def _attn_call(qt, k_proc, vt, *, with_ctx_queries):
    lat_tiles = S // _TQ
    ctx_blk = N_LAT // SC
    gw = ATT_G * HD

    att = pl.pallas_call(
        _attn_latent_kernel,
        out_shape=jax.ShapeDtypeStruct((N_LAT, ATT_QW), BF16),
        grid=(B, ATT_KVH, lat_tiles),
        in_specs=[pl.BlockSpec((gw, _TQ), lambda b, kv, t: (kv, b * lat_tiles + t)),
                  pl.BlockSpec((gw, _TQ),
                               lambda b, kv, t: (kv, b * lat_tiles + jnp.minimum(t + 1, lat_tiles - 1))),
                  pl.BlockSpec((S, HD), lambda b, kv, t: (b, kv)),
                  pl.BlockSpec((SC, HD), lambda b, kv, t: (ctx_blk + b, kv)),
                  pl.BlockSpec((HD, S), lambda b, kv, t: (kv, b)),
                  pl.BlockSpec((HD, SC), lambda b, kv, t: (kv, ctx_blk + b))],
        out_specs=pl.BlockSpec((_TQ, gw), lambda b, kv, t: (b * lat_tiles + t, kv)),
        scratch_shapes=[pltpu.VMEM((2, S + SC, _TQ), F32),
                        pltpu.VMEM((8, _TQ), F32)],
        compiler_params=_cparams(("parallel", "parallel", "arbitrary")),
        name="gqa_attention",
    )(qt, qt, k_proc, k_proc, vt, vt)
    if not with_ctx_queries:
        return att, None
    att_ctx = pl.pallas_call(
        _attn_ctx_kernel,
        out_shape=jax.ShapeDtypeStruct((N_CTX, ATT_QW), BF16),
        grid=(B, ATT_KVH),
        in_specs=[pl.BlockSpec((gw, SC), lambda b, kv: (kv, ctx_blk + b)),
                  pl.BlockSpec((SC, HD), lambda b, kv: (ctx_blk + b, kv)),
                  pl.BlockSpec((HD, SC), lambda b, kv: (kv, ctx_blk + b))],
        out_specs=pl.BlockSpec((SC, gw), lambda b, kv: (b, kv)),
        scratch_shapes=[pltpu.VMEM((2, SC, SC), F32)],
        compiler_params=_cparams(("parallel", "parallel")),
        name="gqa_attention_ctx",
    )(qt, k_proc, vt)
    return att, att_ctx


_ML_CHUNKS = (SC + S) // ML_L
_ML_CTX_CHUNKS = SC // ML_L
_NEG = -1e30


def _log_sigmoid(x):
    return jnp.minimum(x, 0.0) - jnp.log1p(jnp.exp(-jnp.abs(x)))


def _exact_dot_f32(x, a01):
    hi = x.astype(BF16)
    r1 = x - hi.astype(F32)
    mid = r1.astype(BF16)
    lo = (r1 - mid.astype(F32)).astype(BF16)
    return (jnp.dot(hi, a01, preferred_element_type=F32)
            + jnp.dot(mid, a01, preferred_element_type=F32)
            + jnp.dot(lo, a01, preferred_element_type=F32))


_ML_EXT = ML_DV + LANES


_GS_FIELDS = 6
_GS_CHUNKS = 8


def _mlstm_gate_kernel(g_ref, bias_ref, o_ref):
    L = ML_L
    row = lax.broadcasted_iota(jnp.int32, (L, L), 0)
    col = lax.broadcasted_iota(jnp.int32, (L, L), 1)
    lane8 = lax.broadcasted_iota(jnp.int32, (8, L), 1)
    for c in range(_GS_CHUNKS):
        gates_t = (g_ref[c * L:(c + 1) * L, :].astype(F32) + bias_ref[...]).T
        for d in range(2):
            fwd = d == 0
            tri = jnp.where((row <= col) if fwd else (row >= col), 1.0, 0.0).astype(BF16)
            r8 = gates_t[d * 8:(d + 1) * 8, :]
            b = pltpu.roll(_exact_dot_f32(_log_sigmoid(r8), tri), 4, 0)
            u = r8 - b
            g = jnp.broadcast_to(b[:, L - 1:L] if fwd else b[:, 0:1], (8, L))
            w_end = g + u
            m_loc = jnp.broadcast_to(jnp.max(w_end, axis=1, keepdims=True), (8, L))
            e = jnp.exp(w_end - m_loc)
            run_max = u
            step = 1
            while step < L:
                if fwd:
                    shifted = jnp.where(lane8 >= step, pltpu.roll(run_max, step, 1), _NEG)
                else:
                    shifted = jnp.where(lane8 < L - step, pltpu.roll(run_max, L - step, 1), _NEG)
                run_max = jnp.maximum(run_max, shifted)
                step *= 2
            for i, field in enumerate((u, e, run_max, b, g, m_loc)):
                o_ref[c, d, i * 8:(i + 1) * 8, :] = field


def _mlstm_gate_call(p, gate_bias):
    rows = _GS_CHUNKS * ML_L
    return pl.pallas_call(
        _mlstm_gate_kernel,
        out_shape=jax.ShapeDtypeStruct((N_TOK // ML_L, 2, _GS_FIELDS * 8, ML_L), F32),
        grid=(N_TOK // rows,),
        in_specs=[pl.BlockSpec((rows, LANES), lambda i: (i, _P_IF // LANES)),
                  pl.BlockSpec((1, LANES), lambda i: (0, 0))],
        out_specs=pl.BlockSpec((_GS_CHUNKS, 2, _GS_FIELDS * 8, ML_L), lambda i: (i, 0, 0, 0)),
        compiler_params=_cparams(("parallel",)),
        name="mlstm_gates",
    )(p, gate_bias)


def _mlstm_kernel(qf_ref, kf_ref, vf_ref, gsf_ref, qb_ref, kb_ref, vb_ref, gsb_ref,
                  hf_ref, hb_ref, ct_scr, m_scr):
    @pl.when(pl.program_id(1) == 0)
    def _():
        ct_scr[...] = jnp.zeros_like(ct_scr)
        m_scr[...] = jnp.zeros_like(m_scr)

    L = ML_L
    row = lax.broadcasted_iota(jnp.int32, (L, L), 0)
    col = lax.broadcasted_iota(jnp.int32, (L, L), 1)
    ones_tile = jnp.where(lax.broadcasted_iota(jnp.int32, (L, LANES), 1) == 0, 1.0, 0.0).astype(BF16)
    q_scale = ML_DK ** -0.5

    directions = []
    for d, (q_ref, k_ref, v_ref, gs_ref, h_ref) in enumerate(
            ((qf_ref, kf_ref, vf_ref, gsf_ref, hf_ref), (qb_ref, kb_ref, vb_ref, gsb_ref, hb_ref))):
        mask = (col <= row) if d == 0 else (col >= row)
        u, e, run_max, b, g, m_loc = (gs_ref[i * 8:(i + 1) * 8, :] for i in range(_GS_FIELDS))
        m_prev = m_scr[d]
        m_new = jnp.maximum(g + m_prev, m_loc)
        a_old = jnp.exp(g + m_prev - m_new)
        a_loc = jnp.exp(m_loc - m_new)
        m_scr[d] = m_new
        big_m = jnp.maximum(m_prev, run_max)
        inter = jnp.exp(m_prev - big_m)
        floor = jnp.exp(-(b + big_m))
        cols = jnp.concatenate([big_m, inter, floor, jnp.zeros((L - 24, L), F32)], axis=0).T
        directions.append((d, q_ref, k_ref, v_ref, h_ref, mask, u, e, a_old, a_loc, cols))

    for h in range(ML_H):
        for d, q_ref, k_ref, v_ref, h_ref, mask, u, e, a_old, a_loc, cols in directions:
            idx = d * ML_H + h
            ct_prev = ct_scr[idx]
            q = q_ref[:, h * ML_DK:(h + 1) * ML_DK].astype(F32)
            qs = (q * q_scale).astype(BF16)
            qi = (q * (cols[:, 8 + h:9 + h] * q_scale)).astype(BF16)
            kt = k_ref[:, h * ML_DK:(h + 1) * ML_DK].astype(F32).T
            v_ext = jnp.concatenate([v_ref[:, h * ML_DV:(h + 1) * ML_DV], ones_tile], axis=1)

            qk = jnp.dot(qs, kt.astype(BF16), preferred_element_type=F32)
            expo = jnp.where(mask, u[h:h + 1, :] - cols[:, h:h + 1], _NEG)
            w = (jnp.exp(expo) * qk).astype(BF16)
            nd = jnp.dot(jnp.concatenate([w, qi], axis=1),
                         jnp.concatenate([v_ext, ct_prev.astype(BF16)], axis=0),
                         preferred_element_type=F32)
            den = nd[:, ML_DV:ML_DV + 1]
            inv = 1.0 / jnp.maximum(jnp.abs(den), cols[:, 16 + h:17 + h])
            h_ref[:, h * ML_DV:(h + 1) * ML_DV] = (nd[:, :ML_DV] * inv).astype(h_ref.dtype)

            c_loc = jnp.dot((kt * e[h:h + 1, :]).astype(BF16), v_ext, preferred_element_type=F32)
            keep = jnp.concatenate([a_old[h:h + 1, :]] * (_ML_EXT // LANES), axis=1)
            gain = jnp.concatenate([a_loc[h:h + 1, :]] * (_ML_EXT // LANES), axis=1)
            ct_scr[idx] = keep * ct_prev + gain * c_loc


def _mlstm_call(p, gate_stats):
    lat_chunks = S // ML_L
    ctx_blk = N_LAT // ML_L

    def fwd_rb(b, t):
        return jnp.where(t < _ML_CTX_CHUNKS, ctx_blk + b * _ML_CTX_CHUNKS + t,
                         b * lat_chunks + t - _ML_CTX_CHUNKS)

    def rev_rb(b, t):
        return jnp.where(t < _ML_CTX_CHUNKS, ctx_blk + b * _ML_CTX_CHUNKS + (_ML_CTX_CHUNKS - 1 - t),
                         b * lat_chunks + (_ML_CHUNKS - 1 - t))

    def specs(rb, d):
        return [pl.BlockSpec((ML_L, ML_QKW), lambda b, t: (rb(b, t), _P_MLQ // ML_QKW)),
                pl.BlockSpec((ML_L, ML_QKW), lambda b, t: (rb(b, t), _P_MLK // ML_QKW)),
                pl.BlockSpec((ML_L, ML_VW), lambda b, t: (rb(b, t), _P_MLV // ML_VW)),
                pl.BlockSpec((None, None, _GS_FIELDS * 8, ML_L), lambda b, t: (rb(b, t), d, 0, 0))]

    n_chain = 2 * ML_H
    return pl.pallas_call(
        _mlstm_kernel,
        out_shape=(jax.ShapeDtypeStruct((N_TOK, ML_VW), BF16),
                   jax.ShapeDtypeStruct((N_TOK, ML_VW), BF16)),
        grid=(B, _ML_CHUNKS),
        in_specs=specs(fwd_rb, 0) + specs(rev_rb, 1),
        out_specs=(pl.BlockSpec((ML_L, ML_VW), lambda b, t: (fwd_rb(b, t), 0)),
                   pl.BlockSpec((ML_L, ML_VW), lambda b, t: (rev_rb(b, t), 0))),
        scratch_shapes=[pltpu.VMEM((n_chain, ML_DK, _ML_EXT), F32),
                        pltpu.VMEM((2, 8, ML_L), F32)],
        compiler_params=_cparams(("parallel", "arbitrary")),
        name="mlstm",
    )(p, p, p, gate_stats, p, p, p, gate_stats)


def _pool_kernel(u_ref, w_ref, sc_ref, o_ref, mixed_scr, *, seq):
    gi = pl.program_id(1)
    t = lax.broadcasted_iota(jnp.int32, (seq, 1), 0)

    def prev(x, k):
        return jnp.where(t >= k, pltpu.roll(x, k, 0), 0.0)

    def nxt(x, k):
        return jnp.where(t < seq - k, pltpu.roll(x, seq - k, 0), 0.0)

    def centred_residual(doublings):
        def fill():
            half_w = 2 ** doublings
            inv_count = 1.0 / (jnp.minimum(t + half_w, seq) - jnp.maximum(t - half_w, 0)).astype(F32)
            for c in range(POOL_G // LANES):
                cs = slice(c * LANES, (c + 1) * LANES)
                u = u_ref[:, cs].astype(F32)
                before = prev(u, 1)
                after = u
                for k in (1, 2, 4)[:doublings]:
                    before = before + prev(before, k)
                    after = after + nxt(after, k)
                mixed_scr[:, cs] = ((before + after) * inv_count - u).astype(BF16)
        return fill

    for group, window in enumerate(POOL_WINDOWS):
        pl.when(gi == group)(centred_residual(window.bit_length() - 2))
    o_ref[...] = (jnp.dot(mixed_scr[...], w_ref[...], preferred_element_type=F32)
                  * sc_ref[...]).astype(o_ref.dtype)


def _pool_call(p, pool_w, pool_scale, layer, *, seq, first_blk):
    n_groups = len(POOL_WINDOWS)
    return pl.pallas_call(
        functools.partial(_pool_kernel, seq=seq),
        out_shape=jax.ShapeDtypeStruct((B * seq, POOL_W), BF16),
        grid=(B, n_groups),
        in_specs=[pl.BlockSpec((seq, POOL_G), lambda b, g: (first_blk + b, _P_POOL // POOL_G + g)),
                  pl.BlockSpec((None, None, POOL_G, POOL_G), lambda b, g: (layer, g, 0, 0)),
                  pl.BlockSpec((None, 1, POOL_G), lambda b, g: (layer, 0, g))],
        out_specs=pl.BlockSpec((seq, POOL_G), lambda b, g: (b, g)),
        scratch_shapes=[pltpu.VMEM((seq, POOL_G), BF16)],
        compiler_params=_cparams(("parallel", "parallel")),
        name="pool_branch_%d" % seq,
    )(p, pool_w, pool_scale)


def _merge_kernel(*refs, split_ctx, lat_blocks):
    if split_ctx:
        (att_ref, attc_ref, hf_ref, hb_ref, og_ref, pool_ref, poolc_ref, g0_ref, g1_ref, g2_ref,
         wa_ref, wm_ref, wp_ref, wo_ref, mlg_ref, x_ref, xc_ref, ng_ref, gate_ref, o_ref, ml_scr) = refs
        is_ctx = pl.program_id(0) >= lat_blocks
        att = jnp.where(is_ctx, attc_ref[...], att_ref[...])
        pool = jnp.where(is_ctx, poolc_ref[...], pool_ref[...])
        x = jnp.where(is_ctx, xc_ref[...], x_ref[...])
    else:
        (att_ref, hf_ref, hb_ref, og_ref, pool_ref, g0_ref, g1_ref, g2_ref,
         wa_ref, wm_ref, wp_ref, wo_ref, mlg_ref, x_ref, ng_ref, gate_ref, o_ref, ml_scr) = refs
        att, pool, x = att_ref[...], pool_ref[...], x_ref[...]

    for h in range(ML_H):
        cs = slice(h * ML_DV, (h + 1) * ML_DV)
        hs = hf_ref[:, cs].astype(F32) + hb_ref[:, cs].astype(F32)
        ms = jnp.mean(hs * hs, axis=-1, keepdims=True)
        normed = hs * lax.rsqrt(ms + EPS) * mlg_ref[:, cs]
        ml_scr[:, cs] = (normed * og_ref[:, cs].astype(F32)).astype(BF16)

    def branch(g_ref, a, w_ref):
        return g_ref[...].astype(F32) * jnp.dot(a, w_ref[...], preferred_element_type=F32)

    y = branch(g0_ref, att, wa_ref) + branch(g1_ref, ml_scr[...], wm_ref) + branch(g2_ref, pool, wp_ref)
    z = jnp.dot(y.astype(BF16), wo_ref[...], preferred_element_type=F32)
    ms = jnp.mean(z * z, axis=-1, keepdims=True)
    o_ref[...] = x + gate_ref[...] * (z * lax.rsqrt(ms + EPS) * ng_ref[...])


def _merge_call(att, att_ctx, hf, hb, p, pool, pool_ctx, w_up_att, w_up_ml, w_up_pool, w_out, ml_head_g,
                x, x_ctx, ng, gate, layer):
    tm = 256
    split_ctx = att_ctx is not None
    rows = N_TOK if split_ctx else N_LAT
    lat_blocks = N_LAT // tm

    def row_blk(width, col=0):
        return pl.BlockSpec((tm, width), lambda i: (i, col))

    def lat_blk(width):
        if not split_ctx:
            return row_blk(width)
        return pl.BlockSpec((tm, width), lambda i: (jnp.minimum(i, lat_blocks - 1), 0))

    def ctx_blk(width):
        return pl.BlockSpec((tm, width), lambda i: (jnp.maximum(i - lat_blocks, 0), 0))

    def w_blk(k):
        return pl.BlockSpec((None, k, D), lambda i: (layer, 0, 0), pipeline_mode=pl.Buffered(1))

    def with_ctx(spec_arg, ctx_spec_arg):
        return [spec_arg, ctx_spec_arg] if split_ctx else [spec_arg]

    specs_args = (
        with_ctx((lat_blk(ATT_QW), att), (ctx_blk(ATT_QW), att_ctx))
        + [(row_blk(ML_VW), hf), (row_blk(ML_VW), hb), (row_blk(ML_VW, _P_O // ML_VW), p)]
        + with_ctx((lat_blk(POOL_W), pool), (ctx_blk(POOL_W), pool_ctx))
        + [(row_blk(D, _P_GATE // D + branch), p) for branch in range(3)]
        + [(w_blk(ATT_QW), w_up_att), (w_blk(ML_VW), w_up_ml), (w_blk(POOL_W), w_up_pool), (w_blk(D), w_out),
           (pl.BlockSpec((None, 1, ML_VW), lambda i: (layer, 0, 0)), ml_head_g)]
        + with_ctx((lat_blk(D), x), (ctx_blk(D), x_ctx))
        + [(pl.BlockSpec((1, D), lambda i: (0, 0)), ng),
           (pl.BlockSpec((None, 1, D), lambda i: ((i * tm) // S, 0, 0)), gate)])

    return pl.pallas_call(
        functools.partial(_merge_kernel, split_ctx=split_ctx, lat_blocks=lat_blocks),
        out_shape=jax.ShapeDtypeStruct((rows, D), F32),
        grid=(rows // tm,),
        in_specs=[sa[0] for sa in specs_args],
        out_specs=pl.BlockSpec((tm, D), lambda i: (i, 0)),
        scratch_shapes=[pltpu.VMEM((tm, ML_VW), BF16)],
        compiler_params=_cparams(("parallel",)),
        name="merge_out",
    )(*[sa[1] for sa in specs_args])


_N_IF = 4 * ML_H
_W_IN_IF = 2 * ATT_KVW + ML_QKW + ML_VW
_W_IN_SEGMENTS = (
    (0, 2 * ATT_KVW, _P_K),
    (2 * ATT_KVW, ML_QKW, _P_MLK),
    (2 * ATT_KVW + ML_QKW, ML_VW, _P_MLV),
    (_W_IN_IF + _N_IF, ATT_QW, _P_Q),
    (_W_IN_IF + _N_IF + ATT_QW, ML_QKW, _P_MLQ),
    (_W_IN_IF + _N_IF + ATT_QW + ML_QKW, ML_VW, _P_O),
    (_W_IN_IF + _N_IF + ATT_QW + ML_QKW + ML_VW, POOL_W, _P_POOL),
    (_W_IN_IF + _N_IF + ATT_QW + ML_QKW + ML_VW + POOL_W, 3 * D, _P_GATE),
)
_RG_TILE = 512
_RG_TILES = tuple((src + c, dst + c) for src, width, dst in _W_IN_SEGMENTS for c in range(0, width, _RG_TILE))
assert all(width % _RG_TILE == 0 for _, width, _ in _W_IN_SEGMENTS) and _P_W - _P_IF == _RG_TILE


def _regroup_kernel(src_ref, dst_ref, wt_ref, o_ref):
    del src_ref, dst_ref
    t = pl.program_id(1)

    @pl.when(t < len(_RG_TILES))
    def _():
        o_ref[...] = wt_ref[0].T.astype(BF16)

    @pl.when(t == len(_RG_TILES))
    def _():
        lane = lax.broadcasted_iota(jnp.int32, (D, LANES), 1)
        gate_cols = jnp.where(lane < _N_IF, wt_ref[0, 0:LANES, :].T, 0.0).astype(BF16)
        src_l = lax.broadcasted_iota(jnp.int32, (LANES, LANES), 0)
        dst_l = lax.broadcasted_iota(jnp.int32, (LANES, LANES), 1)
        target = ((src_l >> 2) & 1) * (2 * ML_H) + (src_l >> 3) * ML_H + (src_l & (ML_H - 1))
        reorder = jnp.where((src_l < _N_IF) & (dst_l == target), 1.0, 0.0).astype(BF16)
        o_ref[:, 0:LANES] = jnp.dot(gate_cols, reorder, preferred_element_type=F32).astype(BF16)
        o_ref[:, LANES:] = jnp.zeros((D, _RG_TILE - LANES), BF16)


def _regroup_call(w_in):
    wt = jnp.swapaxes(w_in, 1, 2)
    assert all(s % _N_IF == 0 for s, _ in _RG_TILES) and _W_IN_IF % _N_IF == 0
    src = jnp.array([s // _N_IF for s, _ in _RG_TILES] + [_W_IN_IF // _N_IF], jnp.int32)
    dst = jnp.array([d // _RG_TILE for _, d in _RG_TILES] + [_P_IF // _RG_TILE], jnp.int32)
    grid_spec = pltpu.PrefetchScalarGridSpec(
        num_scalar_prefetch=2,
        grid=(DEPTH, len(_RG_TILES) + 1),
        in_specs=[pl.BlockSpec((pl.Element(1), pl.Element(_RG_TILE), pl.Element(D)),
                               lambda l, t, src, dst: (l, src[t] * _N_IF, 0))],
        out_specs=pl.BlockSpec((None, D, _RG_TILE), lambda l, t, src, dst: (l, 0, dst[t])))
    return pl.pallas_call(
        _regroup_kernel,
        out_shape=jax.ShapeDtypeStruct((DEPTH, D, _P_W), BF16),
        grid_spec=grid_spec,
        compiler_params=_cparams(("parallel", "arbitrary")),
        name="w_in_regroup",
    )(src, dst, wt)


def _rope_tables():
    t = jnp.arange(S)
    row = (t // GRID_W).astype(F32)
    colp = (t % GRID_W).astype(F32)
    n_freq = HD // 4
    inv = ROPE_THETA ** (-jnp.arange(n_freq, dtype=F32) / n_freq)
    ar, ac = row[:, None] * inv, colp[:, None] * inv
    cos_t = jnp.concatenate([jnp.cos(ar), jnp.cos(ar), jnp.cos(ac), jnp.cos(ac)], axis=1)
    sin_t = jnp.concatenate([-jnp.sin(ar), jnp.sin(ar), -jnp.sin(ac), jnp.sin(ac)], axis=1)
    return cos_t, sin_t


def kernel(x, c, ctx, c_ctx, w_mod, b_mod, norm_g, w_in, ml_gate_b, qk_norm_g, ml_head_g,
           pool_w, pool_scale, w_up_att, w_up_ml, w_up_pool, w_out, w_ffn_in, w_ffn_out):
    cos_t, sin_t = _rope_tables()
    x_lat = x.reshape(N_LAT, D)
    x_ctx = ctx.reshape(N_CTX, D)
    c8 = jnp.concatenate([c, c_ctx[None], jnp.zeros((8 - B - 1, D), F32)], axis=0)

    w_in_r = _regroup_call(w_in)
    w_out_b = w_out.astype(BF16)
    w_ffn_out_b = w_ffn_out.astype(BF16)
    w_up_att_b, w_up_ml_b, w_up_pool_b = (w.astype(BF16) for w in (w_up_att, w_up_ml, w_up_pool))
    pool_w_b = pool_w.astype(BF16)
    pool_scale_r = pool_scale.reshape(DEPTH, 1, POOL_W)
    ml_head_g_r = ml_head_g.reshape(DEPTH, 1, ML_VW)
    gate_bias = jnp.pad(ml_gate_b.reshape(DEPTH, 1, _N_IF), ((0, 0), (0, 0), (0, LANES - _N_IF)))

    assert DEPTH == 2
    xa = None
    for l in range(DEPTH):
        last = l == DEPTH - 1
        rows = N_LAT if last else N_TOK

        mod = _mod_call(c8, w_mod, b_mod, l).reshape(8, 6, 1, D)
        sh_m, sc_m, gt_m, sh_f, sc_f, gt_f = (mod[:, i] for i in range(6))

        if l == 0:
            p = _in_proj_call(x_lat, x_ctx, norm_g[l, 0][None], sh_m, sc_m, w_in_r, l, ctx_state_only=last)
        else:
            p = _in_proj_call(xa, None, norm_g[l, 0][None], sh_m, sc_m, w_in_r, l, ctx_state_only=last)

        qt, k_proc, vt = _qkprep_call(p, cos_t, sin_t, qk_norm_g[l, 0][None], qk_norm_g[l, 1][None])
        att, att_ctx = _attn_call(qt, k_proc, vt, with_ctx_queries=not last)

        hf, hb = _mlstm_call(p, _mlstm_gate_call(p, gate_bias[l]))

        pool = _pool_call(p, pool_w_b, pool_scale_r, l, seq=S, first_blk=0)
        pool_ctx = None if last else _pool_call(p, pool_w_b, pool_scale_r, l, seq=SC, first_blk=N_LAT // SC)

        xa = _merge_call(att, att_ctx, hf, hb, p, pool, pool_ctx, w_up_att_b, w_up_ml_b, w_up_pool_b,
                         w_out_b, ml_head_g_r, x_lat if l == 0 else xa, x_ctx if l == 0 else None,
                         norm_g[l, 1][None], gt_m, l)

        f = _ffn_in_call(xa, norm_g[l, 2][None], sh_f, sc_f, w_ffn_in, l, rows=rows)
        xa = _resid_call(f, w_ffn_out_b, xa, norm_g[l, 3][None], gt_f, l, rows=rows, tm=256, name="ffn_out")

    return xa.reshape(B, S, D)
```
